```python
import jax, jax.numpy as jnp
from jax import lax
import numpy as np

D_MODEL = 1024
BATCH = 4
SEQ = 4096
DEPTH = 4
DEC_BATCH = 8
DEC_SEQ = 8192
PAST_LEN = 128

HEAD_DIM = 64
A_HEADS = 4
A_WIDTH = A_HEADS * HEAD_DIM
GLA_CHUNK = 64
B_HEADS = 6
B_KV_HEADS = 2
B_GROUP = B_HEADS // B_KV_HEADS
B_HALF_WINDOW = 128
C_PAIRS = ((128, 1), (512, 4), (2048, 16))
C_HEADS_PER_PAIR = 2
C_HEADS = C_HEADS_PER_PAIR * len(C_PAIRS)
IN_SPLITS = (A_WIDTH, A_WIDTH, A_WIDTH, A_WIDTH, A_WIDTH,
             B_HEADS * HEAD_DIM, B_KV_HEADS * HEAD_DIM, B_KV_HEADS * HEAD_DIM,
             C_HEADS * HEAD_DIM, C_HEADS * HEAD_DIM, C_HEADS * HEAD_DIM)
IN_WIDTH = 5 * A_WIDTH + (B_HEADS + 2 * B_KV_HEADS) * HEAD_DIM + 3 * C_HEADS * HEAD_DIM
OUT_WIDTH = A_WIDTH + B_HEADS * HEAD_DIM + C_HEADS_PER_PAIR * HEAD_DIM
D_FF = -(-(8 * D_MODEL) // (3 * 256)) * 256
ROPE_THETA = 10000.0
NORM_EPS = 1e-6
MASK_VALUE = -1e30
GATE_FLOOR = 1e-30

kernel_name = "hymba_style_bidir_hybrid_encoder"


def rmsnorm(x, g):
    xf = x.astype(jnp.float32)
    y = xf * lax.rsqrt(jnp.mean(xf * xf, axis=-1, keepdims=True) + NORM_EPS) * g.astype(jnp.float32)
    return y.astype(x.dtype)


def rope(x, pos):
    half = x.shape[-1] // 2
    inv = ROPE_THETA ** (-jnp.arange(half, dtype=jnp.float32) / half)
    ang = pos.astype(jnp.float32)[:, None] * inv[None, :]
    cos = jnp.cos(ang)[None, :, None, :]
    sin = jnp.sin(ang)[None, :, None, :]
    xf = x.astype(jnp.float32)
    x1, x2 = xf[..., :half], xf[..., half:]
    return jnp.concatenate([x1 * cos - x2 * sin, x2 * cos + x1 * sin], axis=-1).astype(x.dtype)


def gla_chunk_scan(q, k, v, g):
    Bn, T, H, K = q.shape
    V = v.shape[-1]
    n = T // GLA_CHUNK

    def to_chunks(a):
        return a.astype(jnp.float32).reshape(Bn, n, GLA_CHUNK, H, a.shape[-1]).transpose(1, 0, 3, 2, 4)

    qc, kc, vc = to_chunks(q), to_chunks(k), to_chunks(v)
    bc = jnp.cumsum(to_chunks(g), axis=3)
    causal_in_chunk = jnp.tril(jnp.ones((GLA_CHUNK, GLA_CHUNK), dtype=bool))[None, None, :, :, None]

    def step(S, inp):
        q_, k_, v_, b_ = inp
        diff = b_[:, :, :, None, :] - b_[:, :, None, :, :]
        dec = jnp.where(causal_in_chunk, jnp.exp(jnp.where(causal_in_chunk, diff, 0.0)), 0.0)
        A = jnp.einsum('bhtk,bhtsk,bhsk->bhts', q_, dec, k_)
        o = jnp.einsum('bhts,bhsv->bhtv', A, v_) + jnp.einsum('bhtk,bhkv->bhtv', q_ * jnp.exp(b_), S)
        b_last = b_[:, :, -1:, :]
        S = jnp.exp(b_last[:, :, 0, :])[..., None] * S + jnp.einsum('bhsk,bhsv->bhkv', k_ * jnp.exp(b_last - b_), v_)
        return S, o

    S0 = jnp.zeros((Bn, H, K, V), jnp.float32)
    _, o = lax.scan(step, S0, (qc, kc, vc, bc))
    return o.transpose(1, 0, 3, 2, 4).reshape(Bn, T, H, V)


def hgrn2_bidir(q, i, f_fwd, f_bwd, gate, lb, norm_g):
    lb = lb.reshape(2, A_HEADS, HEAD_DIM).astype(jnp.float32)

    def forget(f_raw, lb_d):
        s = jax.nn.sigmoid(f_raw.astype(jnp.float32))
        f = lb_d + (1.0 - lb_d) * s
        return jnp.log(jnp.maximum(f, GATE_FLOOR)), (1.0 - lb_d) * (1.0 - s)

    g_f, k_f = forget(f_fwd, lb[0])
    g_b, k_b = forget(f_bwd, lb[1])
    flip = lambda a: jnp.flip(a, axis=1)
    o = gla_chunk_scan(q, k_f, i, g_f) + flip(gla_chunk_scan(flip(q), flip(k_b), flip(i), flip(g_b)))
    o = rmsnorm(o, norm_g) * jax.nn.silu(gate.astype(jnp.float32))
    return o.reshape(o.shape[0], o.shape[1], A_WIDTH)


def banded_attention(q, k, v, w, sink):
    N, Hk, G, L, dh = q.shape
    nb = -(-L // w)
    pad = nb * w - L
    qb = jnp.pad(q, ((0, 0), (0, 0), (0, 0), (0, pad), (0, 0))).reshape(N, Hk, G, nb, w, dh)
    kp = jnp.pad(k, ((0, 0), (0, 0), (w, pad + w), (0, 0))).reshape(N, Hk, nb + 2, w, dh)
    vp = jnp.pad(v, ((0, 0), (0, 0), (w, pad + w), (0, 0))).reshape(N, Hk, nb + 2, w, dh)
    kw = jnp.concatenate([kp[:, :, :-2], kp[:, :, 1:-1], kp[:, :, 2:]], axis=3)
    vw = jnp.concatenate([vp[:, :, :-2], vp[:, :, 1:-1], vp[:, :, 2:]], axis=3)
    s = jnp.einsum('nhgbqd,nhbkd->nhgbqk', qb, kw, preferred_element_type=jnp.float32) * (dh ** -0.5)
    a = jnp.arange(w)[:, None]
    c = jnp.arange(3 * w)[None, :]
    band = jnp.abs(c - w - a) <= w
    keypos = jnp.arange(nb)[:, None] * w - w + jnp.arange(3 * w)[None, :]
    valid = (keypos >= 0) & (keypos < L)
    mask = band[None, :, :] & valid[:, None, :]
    s = jnp.where(mask, s, MASK_VALUE)
    m = jnp.max(s, axis=-1)
    if sink is not None:
        sk = sink.astype(jnp.float32)[None, :, :, None, None]
        m = jnp.maximum(m, sk)
    p = jnp.where(mask, jnp.exp(s - m[..., None]), 0.0)
    denom = jnp.sum(p, axis=-1)
    if sink is not None:
        denom = denom + jnp.exp(sk - m)
    o = jnp.einsum('nhgbqk,nhbkd->nhgbqd', p, vw.astype(jnp.float32)) / denom[..., None]
    lse = m + jnp.log(denom)
    o = o.reshape(N, Hk, G, nb * w, dh)[:, :, :, :L]
    lse = lse.reshape(N, Hk, G, nb * w)[:, :, :, :L]
    return o, lse


def windowed_gqa(q, k, v, qn_g, kn_g, sink, pos):
    Bn, T = q.shape[:2]
    q = rope(rmsnorm(q, qn_g), pos)
    k = rope(rmsnorm(k, kn_g), pos)
    qg = q.reshape(Bn, T, B_KV_HEADS, B_GROUP, HEAD_DIM).transpose(0, 2, 3, 1, 4)
    o, _ = banded_attention(qg, k.transpose(0, 2, 1, 3), v.transpose(0, 2, 1, 3), B_HALF_WINDOW,
                            sink.reshape(B_KV_HEADS, B_GROUP))
    return o.transpose(0, 3, 1, 2, 4).reshape(Bn, T, B_HEADS * HEAD_DIM)


def to_strided(a, d):
    Bn, H, T = a.shape[:3]
    rest = a.shape[3:]
    a = jnp.moveaxis(a.reshape(Bn, H, T // d, d, *rest), 3, 1)
    return a.reshape(Bn * d, H, T // d, *rest)


def from_strided(a, Bn, d):
    H, L = a.shape[1:3]
    rest = a.shape[3:]
    a = jnp.moveaxis(a.reshape(Bn, d, H, L, *rest), 1, 3)
    return a.reshape(Bn, H, L * d, *rest)


def dilated_attention(q, k, v, qn_g, kn_g, pos):
    Bn, T = q.shape[:2]
    q = rope(rmsnorm(q, qn_g), pos).transpose(0, 2, 1, 3)
    k = rope(rmsnorm(k, kn_g), pos).transpose(0, 2, 1, 3)
    v = v.transpose(0, 2, 1, 3)
    outs, lses = [], []
    for g, (window, dil) in enumerate(C_PAIRS):
        sl = slice(g * C_HEADS_PER_PAIR, (g + 1) * C_HEADS_PER_PAIR)
        qd, kd, vd = to_strided(q[:, sl], dil), to_strided(k[:, sl], dil), to_strided(v[:, sl], dil)
        o, lse = banded_attention(qd[:, :, None], kd, vd, window // (2 * dil), None)
        outs.append(from_strided(o[:, :, 0], Bn, dil))
        lses.append(from_strided(lse[:, :, 0], Bn, dil))
    alpha = jax.nn.softmax(jnp.stack(lses), axis=0)
    o = jnp.einsum('gbhtd,gbht->bhtd', jnp.stack(outs), alpha)
    return o.transpose(0, 2, 1, 3).reshape(Bn, T, C_HEADS_PER_PAIR * HEAD_DIM)


def encoder_layer(x, n1, w_in_l, lb_l, a_g, bq_g, bk_g, sink_l, cq_g, ck_g, w_out_l, n2, wg, wu, wd):
    Bn, T, _ = x.shape
    xn = rmsnorm(x, n1)
    proj = xn @ w_in_l
    offsets = np.cumsum(np.array(IN_SPLITS))[:-1]
    aq, aff, afb, ai, ag, bq, bk, bv, cq, ck, cv = jnp.split(proj, offsets, axis=-1)
    hd = lambda a: a.reshape(Bn, T, -1, HEAD_DIM)
    pos = jnp.arange(T)
    ya = hgrn2_bidir(hd(aq), hd(ai), hd(aff), hd(afb), hd(ag), lb_l, a_g).astype(x.dtype)
    yb = windowed_gqa(hd(bq), hd(bk), hd(bv), bq_g, bk_g, sink_l, pos).astype(x.dtype)
    yc = dilated_attention(hd(cq), hd(ck), hd(cv), cq_g, ck_g, pos).astype(x.dtype)
    h = x + jnp.concatenate([ya, yb, yc], axis=-1) @ w_out_l
    hn = rmsnorm(h, n2)
    return h + (jax.nn.silu(hn @ wg) * (hn @ wu)) @ wd


def setup_inputs(seed: int = 0) -> dict:
    key = jax.random.key(seed)
    ks = jax.random.split(key, 16)
    nrm = lambda k, s: jax.random.normal(k, s, jnp.float32)
    return {
        "x_prompt": nrm(ks[0], (BATCH, SEQ, D_MODEL)),
        "x_sample": nrm(ks[1], (DEC_BATCH, DEC_SEQ, D_MODEL)),
        "norm1_g": 1.0 + 0.02 * nrm(ks[2], (DEPTH, D_MODEL)),
        "w_in": nrm(ks[3], (DEPTH, D_MODEL, IN_WIDTH)) * D_MODEL ** -0.5,
        "lb_raw": 0.1 * nrm(ks[4], (DEPTH, 2, A_WIDTH)),
        "a_norm_g": 1.0 + 0.02 * nrm(ks[5], (DEPTH, HEAD_DIM)),
        "b_qn_g": 1.0 + 0.02 * nrm(ks[6], (DEPTH, HEAD_DIM)),
        "b_kn_g": 1.0 + 0.02 * nrm(ks[7], (DEPTH, HEAD_DIM)),
        "b_sink": 0.5 * nrm(ks[8], (DEPTH, B_HEADS)),
        "c_qn_g": 1.0 + 0.02 * nrm(ks[9], (DEPTH, HEAD_DIM)),
        "c_kn_g": 1.0 + 0.02 * nrm(ks[10], (DEPTH, HEAD_DIM)),
        "w_out": nrm(ks[11], (DEPTH, OUT_WIDTH, D_MODEL)) * (0.5 * OUT_WIDTH ** -0.5),
        "norm2_g": 1.0 + 0.02 * nrm(ks[12], (DEPTH, D_MODEL)),
        "w_gate": nrm(ks[13], (DEPTH, D_MODEL, D_FF)) * D_MODEL ** -0.5,
        "w_up": nrm(ks[14], (DEPTH, D_MODEL, D_FF)) * D_MODEL ** -0.5,
        "w_down": nrm(ks[15], (DEPTH, D_FF, D_MODEL)) * (0.5 * D_FF ** -0.5),
    }


def reference(x_prompt, x_sample, norm1_g, w_in, lb_raw, a_norm_g, b_qn_g, b_kn_g, b_sink,
              c_qn_g, c_kn_g, w_out, norm2_g, w_gate, w_up, w_down):
    lb_sm = jax.nn.softmax(lb_raw.astype(jnp.float32), axis=0)
    lb = jnp.cumsum(lb_sm, axis=0) - lb_sm[:1]

    def trunk(x):
        for l in range(DEPTH):
            x = encoder_layer(x, norm1_g[l], w_in[l], lb[l], a_norm_g[l], b_qn_g[l], b_kn_g[l], b_sink[l],
                              c_qn_g[l], c_kn_g[l], w_out[l], norm2_g[l], w_gate[l], w_up[l], w_down[l])
        return x

    y_prompt = trunk(x_prompt)
    y_sample = trunk(x_sample)
    return (y_prompt, y_sample)
```

```python
import functools

import jax
import jax.numpy as jnp
import numpy as np
from jax import lax
from jax.experimental import pallas as pl
from jax.experimental.pallas import tpu as pltpu

F32 = jnp.float32
BF16 = jnp.bfloat16

D_MODEL = 1024
HEAD_DIM = 64
LANES = 128
A_WIDTH = 256
A_COLS = 5 * A_WIDTH
BQ_COL, BK_COL, BV_COL = 1280, 1664, 1792
CQ_COL, CK_COL, CV_COL = 1920, 2304, 2688
B_HALF_WINDOW = 128
C_HALF_WINDOW = 64
C_DILATIONS = (1, 4, 16)
D_FF = 2816
FF_CHUNK = 256
ROPE_THETA = 10000.0
NORM_EPS = 1e-6
MASK_VALUE = -1e30
GATE_FLOOR = 1e-30
GLA_CHUNK = 64
GLA_SAFE_RANGE = 80.0

TOKEN_TILE = 512
HGRN_TILE = 512
ATTN_TILE = 512
VMEM_LIMIT = 56 * 1024 * 1024


def _dot(a, b):
    return jnp.dot(a, b, preferred_element_type=F32)


def _dot_nt(a, b):
    return lax.dot_general(a, b, (((1,), (1,)), ((), ())), preferred_element_type=F32)


def _dot_tn(a, b):
    return lax.dot_general(a, b, (((0,), (0,)), ((), ())), preferred_element_type=F32)


def _head_blockdiag(n):
    r = lax.broadcasted_iota(jnp.int32, (n, n), 0) >> 6
    c = lax.broadcasted_iota(jnp.int32, (n, n), 1) >> 6
    return (r == c).astype(BF16)


def _head_sum(x, bd):
    hi = x.astype(BF16)
    lo = (x - hi.astype(F32)).astype(BF16)
    return _dot(hi, bd) + _dot(lo, bd)


def _head_rmsnorm(y, g, bd):
    return y * lax.rsqrt(_head_sum(y * y, bd) * (1.0 / HEAD_DIM) + NORM_EPS) * g


def _silu(x):
    return x * jax.nn.sigmoid(x)


def _proj_kernel(x_ref, g1_ref, w_ref, cos_ref, sin_ref, bqg_ref, bkg_ref, cqg_ref, ckg_ref,
                 a_ref, bq_ref, bk_ref, bv_ref,
                 cq1_ref, ck1_ref, cv1_ref, cq4_ref, ck4_ref, cv4_ref, cq16_ref, ck16_ref, cv16_ref,
                 scr_ref):
    tm = x_ref.shape[1]
    x = x_ref[0]
    xn = (x * lax.rsqrt(jnp.mean(x * x, axis=-1, keepdims=True) + NORM_EPS) * g1_ref[...]).astype(BF16)
    a_ref[0] = _dot(xn, w_ref[:, 0:A_COLS])

    bd = _head_blockdiag(LANES)
    cos = cos_ref[...]
    sin = sin_ref[...]
    lane = lax.broadcasted_iota(jnp.int32, (1, LANES), 1)
    first_half = (lane & (HEAD_DIM - 1)) < (HEAD_DIM // 2)

    def proj(col):
        return _dot(xn, w_ref[:, col:col + LANES])

    def norm_rope(col, g_ref):
        y = _head_rmsnorm(proj(col), g_ref[...], bd)
        partner = jnp.where(first_half, pltpu.roll(y, LANES - HEAD_DIM // 2, 1), pltpu.roll(y, HEAD_DIM // 2, 1))
        return y * cos + partner * sin

    scale = HEAD_DIM ** -0.5
    for j in range(3):
        bq_ref[0, :, j * LANES:(j + 1) * LANES] = (norm_rope(BQ_COL + j * LANES, bqg_ref) * scale).astype(BF16)
    bk = norm_rope(BK_COL, bkg_ref)
    bk_ref[0, :, 0:LANES] = bk.astype(BF16)
    bk_ref[0, :, LANES:2 * LANES] = pltpu.roll(bk, HEAD_DIM, 1).astype(BF16)
    bv = proj(BV_COL)
    bv_ref[0, :, 0:LANES] = bv.astype(BF16)
    bv_ref[0, :, LANES:2 * LANES] = pltpu.roll(bv, HEAD_DIM, 1).astype(BF16)

    def store_dilated(ref, y, d):
        if d == 1:
            ref[0] = y.astype(BF16)
            return
        scr_ref[...] = y
        for r in range(d):
            ref[0, r] = scr_ref[pl.ds(r, tm // d, stride=d), :].astype(BF16)

    c_refs = ((cq1_ref, ck1_ref, cv1_ref), (cq4_ref, ck4_ref, cv4_ref), (cq16_ref, ck16_ref, cv16_ref))
    for p, d in enumerate(C_DILATIONS):
        q_ref, k_ref, v_ref = c_refs[p]
        store_dilated(q_ref, norm_rope(CQ_COL + p * LANES, cqg_ref) * scale, d)
        store_dilated(k_ref, norm_rope(CK_COL + p * LANES, ckg_ref), d)
        store_dilated(v_ref, proj(CV_COL + p * LANES), d)


def _rope_tables(t_len):
    half = HEAD_DIM // 2
    inv = ROPE_THETA ** (-jnp.arange(half, dtype=F32) / half)
    ang = jnp.arange(t_len).astype(F32)[:, None] * inv[None, :]
    cos = jnp.tile(jnp.cos(ang), (1, 2 * LANES // HEAD_DIM))
    sin = jnp.sin(ang)
    sin = jnp.tile(jnp.concatenate([-sin, sin], axis=1), (1, LANES // HEAD_DIM))
    return cos, sin


def _proj_call(x, g1, w_in, cos, sin, bqg, bkg, cqg, ckg):
    n, t, _ = x.shape
    tm = TOKEN_TILE
    grid = (n, t // tm)
    tok = lambda width: pl.BlockSpec((1, tm, width), lambda b, i: (b, i, 0))
    const = lambda shape: pl.BlockSpec(shape, lambda b, i: (0,) * len(shape))
    dil = lambda d: pl.BlockSpec((1, d, tm // d, LANES), lambda b, i: (b, 0, i, 0))
    out_shape = [jax.ShapeDtypeStruct((n, t, A_COLS), F32),
                 jax.ShapeDtypeStruct((n, t, 3 * LANES), BF16),
                 jax.ShapeDtypeStruct((n, t, 2 * LANES), BF16),
                 jax.ShapeDtypeStruct((n, t, 2 * LANES), BF16)]
    out_specs = [tok(A_COLS), tok(3 * LANES), tok(2 * LANES), tok(2 * LANES)]
    for d in C_DILATIONS:
        for _ in range(3):
            if d == 1:
                out_shape.append(jax.ShapeDtypeStruct((n, t, LANES), BF16))
                out_specs.append(tok(LANES))
            else:
                out_shape.append(jax.ShapeDtypeStruct((n, d, t // d, LANES), BF16))
                out_specs.append(dil(d))
    rope_spec = pl.BlockSpec((tm, LANES), lambda b, i: (i, 0))
    return pl.pallas_call(
        _proj_kernel,
        grid=grid,
        in_specs=[tok(D_MODEL), const((1, D_MODEL)), const(w_in.shape), rope_spec, rope_spec,
                  const((1, LANES)), const((1, LANES)), const((1, LANES)), const((1, LANES))],
        out_specs=out_specs,
        out_shape=out_shape,
        scratch_shapes=[pltpu.VMEM((tm, LANES), F32)],
        compiler_params=pltpu.CompilerParams(dimension_semantics=("arbitrary", "arbitrary"),
                                             vmem_limit_bytes=VMEM_LIMIT),
        name="proj",
    )(x, g1, w_in, cos, sin, bqg, bkg, cqg, ckg)


def _hgrn_chunk(q, f_raw, v, lb, state_ref, intra_ref, tb_ref, tk_ref, tv_ref, reverse):
    c = GLA_CHUNK
    s = jax.nn.sigmoid(f_raw)
    f = lb + (1.0 - lb) * s
    g = jnp.log(jnp.maximum(f, GATE_FLOOR))
    kk = (1.0 - lb) * (1.0 - s)

    row = lax.broadcasted_iota(jnp.int32, (c, c), 0)
    col = lax.broadcasted_iota(jnp.int32, (c, c), 1)
    tri = ((col >= row) if reverse else (col <= row)).astype(BF16)
    g1 = g.astype(BF16)
    r1 = g - g1.astype(F32)
    g2 = r1.astype(BF16)
    g3 = (r1 - g2.astype(F32)).astype(BF16)
    b = _dot(tri, g1) + _dot(tri, g2) + _dot(tri, g3)
    total = b[0:1] if reverse else b[c - 1:c]
    mid = b[c // 2:c // 2 + 1] if reverse else b[c // 2 - 1:c // 2]

    lane_head = lax.broadcasted_iota(jnp.int32, (1, A_WIDTH), 1) >> 6
    v16 = v.astype(BF16)

    state = state_ref[...]
    o = _dot_nt((q * jnp.exp(b)).astype(BF16), state.astype(BF16))
    upd = _dot_tn(v16, (kk * jnp.exp(total - b)).astype(BF16))
    srow = lax.broadcasted_iota(jnp.int32, (A_WIDTH, A_WIDTH), 0) >> 6
    scol = lax.broadcasted_iota(jnp.int32, (A_WIDTH, A_WIDTH), 1) >> 6
    state_ref[...] = state * jnp.exp(total) + jnp.where(srow == scol, upd, 0.0)

    in_range = jnp.max(jnp.abs(b - mid)) <= GLA_SAFE_RANGE

    @pl.when(in_range)
    def _():
        qp = q * jnp.exp(b - mid)
        kp = (kk * jnp.exp(mid - b)).astype(BF16)
        q_stack = jnp.concatenate([jnp.where(lane_head == h, qp, 0.0) for h in range(4)], axis=0).astype(BF16)
        a = _dot_nt(q_stack, kp)
        t_idx = lax.broadcasted_iota(jnp.int32, (4 * c, c), 0) & (c - 1)
        s_idx = lax.broadcasted_iota(jnp.int32, (4 * c, c), 1)
        a = jnp.where((s_idx >= t_idx) if reverse else (s_idx <= t_idx), a, 0.0)
        o2 = _dot(a.astype(BF16), v16)
        acc = jnp.where(lane_head == 0, o2[0:c], 0.0)
        for h in range(1, 4):
            acc = acc + jnp.where(lane_head == h, o2[h * c:(h + 1) * c], 0.0)
        intra_ref[...] = acc

    @pl.when(jnp.logical_not(in_range))
    def _():
        tb_ref[...] = b
        tk_ref[...] = kk
        tv_ref[...] = v
        bd = _head_blockdiag(A_WIDTH)
        t_col = lax.broadcasted_iota(jnp.int32, (c, 1), 0)

        def body(si, acc):
            b_s = tb_ref[pl.ds(si, 1), :]
            dec = jnp.exp(jnp.minimum(b - b_s, 0.0))
            valid = (t_col <= si) if reverse else (t_col >= si)
            p = jnp.where(valid, q * dec * tk_ref[pl.ds(si, 1), :], 0.0)
            return acc + _dot(p.astype(BF16), bd) * tv_ref[pl.ds(si, 1), :]

        intra_ref[...] = lax.fori_loop(0, c, body, jnp.zeros((c, A_WIDTH), F32))

    return o + intra_ref[...]


def _hgrn_kernel(lb_ref, qf_ref, ff_ref, vf_ref, qb_ref, fb_ref, vb_ref, of_ref, ob_ref,
                 sf_ref, sb_ref, intra_ref, tb_ref, tk_ref, tv_ref, *, layer, n_chunks):
    @pl.when(pl.program_id(1) == 0)
    def _():
        sf_ref[...] = jnp.zeros_like(sf_ref)
        sb_ref[...] = jnp.zeros_like(sb_ref)

    raw = lb_ref[...]
    depth = raw.shape[0]
    mx = raw[0:1]
    for i in range(1, depth):
        mx = jnp.maximum(mx, raw[i:i + 1])
    e = jnp.exp(raw - mx)
    den = e[0:1]
    for i in range(1, depth):
        den = den + e[i:i + 1]
    sm = e / den
    cs = sm[0:1]
    for i in range(1, layer + 1):
        cs = cs + sm[i:i + 1]
    lb = cs - sm[0:1]
    lb_f = lb[:, 0:A_WIDTH]
    lb_b = lb[:, A_WIDTH:2 * A_WIDTH]
    tmp = (intra_ref, tb_ref, tk_ref, tv_ref)

    def step(ci, carry):
        r0 = pl.multiple_of(ci * GLA_CHUNK, GLA_CHUNK)
        rows = pl.ds(r0, GLA_CHUNK)
        of_ref[0, rows, :] = _hgrn_chunk(qf_ref[0, rows, :], ff_ref[0, rows, :], vf_ref[0, rows, :], lb_f,
                                         sf_ref, *tmp, reverse=False)
        r1 = pl.multiple_of((n_chunks - 1 - ci) * GLA_CHUNK, GLA_CHUNK)
        rows = pl.ds(r1, GLA_CHUNK)
        ob_ref[0, rows, :] = _hgrn_chunk(qb_ref[0, rows, :], fb_ref[0, rows, :], vb_ref[0, rows, :], lb_b,
                                         sb_ref, *tmp, reverse=True)
        return carry

    lax.fori_loop(0, n_chunks, step, 0)


def _hgrn_call(a_raw, lb_raw2, layer):
    n, t, _ = a_raw.shape
    tt = HGRN_TILE
    nt = t // tt
    fwd = lambda cb: pl.BlockSpec((1, tt, A_WIDTH), lambda b, i: (b, i, cb))
    bwd = lambda cb: pl.BlockSpec((1, tt, A_WIDTH), lambda b, i: (b, nt - 1 - i, cb))
    return pl.pallas_call(
        functools.partial(_hgrn_kernel, layer=layer, n_chunks=tt // GLA_CHUNK),
        grid=(n, nt),
        in_specs=[pl.BlockSpec(lb_raw2.shape, lambda b, i: (0, 0)),
                  fwd(0), fwd(1), fwd(3), bwd(0), bwd(2), bwd(3)],
        out_specs=[fwd(0), bwd(0)],
        out_shape=[jax.ShapeDtypeStruct((n, t, A_WIDTH), F32)] * 2,
        scratch_shapes=[pltpu.VMEM((A_WIDTH, A_WIDTH), F32), pltpu.VMEM((A_WIDTH, A_WIDTH), F32)]
        + [pltpu.VMEM((GLA_CHUNK, A_WIDTH), F32)] * 4,
        compiler_params=pltpu.CompilerParams(dimension_semantics=("arbitrary", "arbitrary"),
                                             vmem_limit_bytes=VMEM_LIMIT),
        name="hgrn",
    )(lb_raw2, a_raw, a_raw, a_raw, a_raw, a_raw, a_raw)


def _attn_kernel(*refs, w, seq_len, heads, use_sink, want_lse):
    q_ref, kp_ref, kc_ref, kn_ref, vp_ref, vc_ref, vn_ref = refs[:7]
    rest = list(refs[7:])
    sink_ref = rest.pop(0) if use_sink else None
    o_ref = rest.pop(0)
    lse_ref = rest.pop(0) if want_lse else None

    tr = q_ref.shape[1]
    sb_rows, sb_keys = 2 * w, 4 * w
    t0 = pl.program_id(1) * tr
    k_ext = jnp.concatenate([kp_ref[0], kc_ref[0], kn_ref[0]], axis=0)
    v_ext = jnp.concatenate([vp_ref[0], vc_ref[0], vn_ref[0]], axis=0)
    lane_half = lax.broadcasted_iota(jnp.int32, (1, LANES), 1) >> 6

    kv = {}
    for blk_heads in heads:
        for half, kv_blk in blk_heads:
            if (half, kv_blk) not in kv:
                sel = lane_half == half
                kv[(half, kv_blk)] = (jnp.where(sel, k_ext[:, kv_blk * LANES:(kv_blk + 1) * LANES], 0),
                                      jnp.where(sel, v_ext[:, kv_blk * LANES:(kv_blk + 1) * LANES], 0))

    row = lax.broadcasted_iota(jnp.int32, (sb_rows, sb_keys), 0)
    col = lax.broadcasted_iota(jnp.int32, (sb_rows, sb_keys), 1)
    band = (col >= row) & (col - row <= 2 * w)
    for sb in range(tr // sb_rows):
        r0 = sb * sb_rows
        key_pos = col + (t0 - w + r0)
        mask = band & (key_pos >= 0) & (key_pos < seq_len)
        for j, blk_heads in enumerate(heads):
            q = q_ref[0, r0:r0 + sb_rows, j * LANES:(j + 1) * LANES]
            out = None
            lse_out = None
            for e, (half, kv_blk) in enumerate(blk_heads):
                k_h, v_h = kv[(half, kv_blk)]
                s = _dot_nt(q, k_h[r0:r0 + sb_keys])
                s = jnp.where(mask, s, MASK_VALUE)
                m = jnp.max(s, axis=-1, keepdims=True)
                if use_sink:
                    sink = sink_ref[2 * j + e]
                    m = jnp.maximum(m, sink)
                p = jnp.exp(s - m)
                den = jnp.sum(p, axis=-1, keepdims=True)
                if use_sink:
                    den = den + jnp.exp(sink - m)
                o = _dot(p.astype(BF16), v_h[r0:r0 + sb_keys]) / den
                out = o if out is None else out + o
                if want_lse:
                    lse = m + jnp.log(den)
                    lse_out = lse if lse_out is None else jnp.where(lane_half == half, lse, lse_out)
            o_ref[0, r0:r0 + sb_rows, j * LANES:(j + 1) * LANES] = out.astype(o_ref.dtype)
            if want_lse:
                lse_ref[0, r0:r0 + sb_rows, :] = jnp.broadcast_to(lse_out, (sb_rows, LANES))


def _attn_call(q, k, v, sink, *, w, heads, want_lse, name):
    n, seq_len, qw = q.shape
    kw = k.shape[-1]
    tr = min(ATTN_TILE, seq_len)
    per = tr // w
    last = seq_len // w - 1
    cur = lambda width: pl.BlockSpec((1, tr, width), lambda b, i: (b, i, 0))
    prev = pl.BlockSpec((1, w, kw), lambda b, i: (b, jnp.maximum(i * per - 1, 0), 0))
    nxt = pl.BlockSpec((1, w, kw), lambda b, i: (b, jnp.minimum((i + 1) * per, last), 0))
    in_specs = [cur(qw), prev, cur(kw), nxt, prev, cur(kw), nxt]
    args = [q, k, k, k, v, v, v]
    if sink is not None:
        in_specs.append(pl.BlockSpec(memory_space=pltpu.SMEM))
        args.append(sink)
    out_shape = [jax.ShapeDtypeStruct((n, seq_len, qw), BF16)]
    out_specs = [cur(qw)]
    if want_lse:
        out_shape.append(jax.ShapeDtypeStruct((n, seq_len, LANES), F32))
        out_specs.append(cur(LANES))
    return pl.pallas_call(
        functools.partial(_attn_kernel, w=w, seq_len=seq_len, heads=heads, use_sink=sink is not None,
                          want_lse=want_lse),
        grid=(n, seq_len // tr),
        in_specs=in_specs,
        out_specs=out_specs,
        out_shape=out_shape,
        compiler_params=pltpu.CompilerParams(dimension_semantics=("arbitrary", "arbitrary"),
                                             vmem_limit_bytes=VMEM_LIMIT),
        name=name,
    )(*args)


_B_HEADS = tuple(tuple((h % 2, 0 if (h // 3) == (h % 2) else 1) for h in (2 * j, 2 * j + 1)) for j in range(3))
_C_HEADS = (((0, 0), (1, 0)),)


def _outffn_kernel(x_ref, of_ref, ob_ref, gate_ref, ag_ref, yb_ref,
                   o1_ref, l1_ref, o4_ref, l4_ref, o16_ref, l16_ref,
                   wout_ref, n2_ref, wg_ref, wu_ref, wd_ref,
                   y_ref, s4o_ref, s4l_ref, s16o_ref, s16l_ref, hn_ref, acc_ref):
    tm = x_ref.shape[1]
    bd = _head_blockdiag(LANES)
    ya = []
    for j in range(A_WIDTH // LANES):
        cols = slice(j * LANES, (j + 1) * LANES)
        o = of_ref[0, :, cols] + ob_ref[0, :, cols]
        ya.append(_head_rmsnorm(o, ag_ref[...], bd) * _silu(gate_ref[0, :, cols]))
    ya = jnp.concatenate(ya, axis=1).astype(BF16)

    def interleave(o_ref, l_ref, so_ref, sl_ref, d):
        for r in range(d):
            so_ref[pl.ds(r, tm // d, stride=d), :] = o_ref[0, r].astype(F32)
            sl_ref[pl.ds(r, tm // d, stride=d), :] = l_ref[0, r]
        return so_ref[...], sl_ref[...]

    o1, l1 = o1_ref[0].astype(F32), l1_ref[0]
    o4, l4 = interleave(o4_ref, l4_ref, s4o_ref, s4l_ref, 4)
    o16, l16 = interleave(o16_ref, l16_ref, s16o_ref, s16l_ref, 16)
    m = jnp.maximum(jnp.maximum(l1, l4), l16)
    e1, e4, e16 = jnp.exp(l1 - m), jnp.exp(l4 - m), jnp.exp(l16 - m)
    yc = ((e1 * o1 + e4 * o4 + e16 * o16) / (e1 + e4 + e16)).astype(BF16)

    h = (x_ref[0] + _dot(ya, wout_ref[0:A_WIDTH]) + _dot(yb_ref[0], wout_ref[A_WIDTH:A_WIDTH + 3 * LANES])
         + _dot(yc, wout_ref[A_WIDTH + 3 * LANES:A_WIDTH + 4 * LANES]))
    hn_ref[...] = (h * lax.rsqrt(jnp.mean(h * h, axis=-1, keepdims=True) + NORM_EPS) * n2_ref[...]).astype(BF16)
    acc_ref[...] = h

    def ff(ci, carry):
        hn = hn_ref[...]
        act = (_silu(_dot(hn, wg_ref[ci])) * _dot(hn, wu_ref[ci])).astype(BF16)
        acc_ref[...] += _dot(act, wd_ref[ci])
        return carry

    lax.fori_loop(0, D_FF // FF_CHUNK, ff, 0)
    y_ref[0] = acc_ref[...]


def _outffn_call(x, o_f, o_b, a_raw, ag, yb, c_out, w_out, n2, wg, wu, wd):
    n, t, _ = x.shape
    tm = TOKEN_TILE
    tok = lambda width, cb=0: pl.BlockSpec((1, tm, width), lambda b, i: (b, i, cb))
    dil = lambda d: pl.BlockSpec((1, d, tm // d, LANES), lambda b, i: (b, 0, i, 0))
    const = lambda shape: pl.BlockSpec(shape, lambda b, i: (0,) * len(shape), pipeline_mode=pl.Buffered(1))
    (o1, l1), (o4, l4), (o16, l16) = c_out
    return pl.pallas_call(
        _outffn_kernel,
        grid=(n, t // tm),
        in_specs=[tok(D_MODEL), tok(A_WIDTH), tok(A_WIDTH), tok(A_WIDTH, 4), const((1, LANES)), tok(3 * LANES),
                  tok(LANES), tok(LANES), dil(4), dil(4), dil(16), dil(16),
                  const(w_out.shape), const((1, D_MODEL)), const(wg.shape), const(wu.shape), const(wd.shape)],
        out_specs=tok(D_MODEL),
        out_shape=jax.ShapeDtypeStruct((n, t, D_MODEL), F32),
        scratch_shapes=[pltpu.VMEM((tm, LANES), F32)] * 4
        + [pltpu.VMEM((tm, D_MODEL), BF16), pltpu.VMEM((tm, D_MODEL), F32)],
        compiler_params=pltpu.CompilerParams(dimension_semantics=("arbitrary", "arbitrary"),
                                             vmem_limit_bytes=VMEM_LIMIT),
        name="outffn",
    )(x, o_f, o_b, a_raw, ag, yb, o1, l1, o4, l4, o16, l16, w_out, n2, wg, wu, wd)


def _layer(x, p, layer, cos, sin):
    n, t, _ = x.shape
    outs = _proj_call(x, p["n1"], p["w_in"], cos, sin, p["bqg"], p["bkg"], p["cqg"], p["ckg"])
    a_raw, bq, bk, bv = outs[:4]
    o_f, o_b = _hgrn_call(a_raw, p["lb_raw"], layer)
    (yb,) = _attn_call(bq, bk, bv, p["sink"], w=B_HALF_WINDOW, heads=_B_HEADS, want_lse=False, name="attn_window")
    c_out = []
    for i, d in enumerate(C_DILATIONS):
        q, k, v = (a.reshape(n * d, t // d, LANES) for a in outs[4 + 3 * i:7 + 3 * i])
        o, lse = _attn_call(q, k, v, None, w=C_HALF_WINDOW, heads=_C_HEADS, want_lse=True, name=f"attn_dil{d}")
        shape = (n, t, LANES) if d == 1 else (n, d, t // d, LANES)
        c_out.append((o.reshape(shape), lse.reshape(shape)))
    return _outffn_call(x, o_f, o_b, a_raw, p["ag"], yb, c_out, p["w_out"], p["n2"], p["wg"], p["wu"], p["wd"])


def _layer_params(l, norm1_g, w_in, lb_raw, a_norm_g, b_qn_g, b_kn_g, b_sink, c_qn_g, c_kn_g, w_out, norm2_g,
                  w_gate, w_up, w_down):
    pair = lambda g: jnp.tile(g[l].astype(F32), LANES // HEAD_DIM)[None, :]
    chunks = D_FF // FF_CHUNK
    return {
        "n1": norm1_g[l].astype(F32)[None, :],
        "w_in": w_in[l].astype(BF16),
        "lb_raw": lb_raw.astype(F32).reshape(lb_raw.shape[0], 2 * A_WIDTH),
        "ag": pair(a_norm_g), "bqg": pair(b_qn_g), "bkg": pair(b_kn_g), "cqg": pair(c_qn_g), "ckg": pair(c_kn_g),
        "sink": b_sink[l].astype(F32),
        "w_out": w_out[l].astype(BF16),
        "n2": norm2_g[l].astype(F32)[None, :],
        "wg": w_gate[l].astype(BF16).reshape(D_MODEL, chunks, FF_CHUNK).transpose(1, 0, 2),
        "wu": w_up[l].astype(BF16).reshape(D_MODEL, chunks, FF_CHUNK).transpose(1, 0, 2),
        "wd": w_down[l].astype(BF16).reshape(chunks, FF_CHUNK, D_MODEL),
    }


def kernel(x_prompt, x_sample, norm1_g, w_in, lb_raw, a_norm_g, b_qn_g, b_kn_g, b_sink, c_qn_g, c_kn_g, w_out,
           norm2_g, w_gate, w_up, w_down):
    depth = w_in.shape[0]
    params = [_layer_params(l, norm1_g, w_in, lb_raw, a_norm_g, b_qn_g, b_kn_g, b_sink, c_qn_g, c_kn_g, w_out,
                            norm2_g, w_gate, w_up, w_down) for l in range(depth)]

    def trunk(x):
        cos, sin = _rope_tables(x.shape[1])
        for l in range(depth):
            x = _layer(x, params[l], l, cos, sin)
        return x

    return (trunk(x_prompt), trunk(x_sample))
```

```python
import functools

import jax
import jax.numpy as jnp
import numpy as np
from jax import lax
from jax.experimental import pallas as pl
from jax.experimental.pallas import tpu as pltpu

F32 = jnp.float32
BF16 = jnp.bfloat16

D_MODEL = 1024
HEAD_DIM = 64
LANES = 128
A_WIDTH = 256
A_COLS = 5 * A_WIDTH
BQ_COL, BK_COL, BV_COL = 1280, 1664, 1792
CQ_COL, CK_COL, CV_COL = 1920, 2304, 2688
B_HALF_WINDOW = 128
C_HALF_WINDOW = 64
C_DILATIONS = (1, 4, 16)
D_FF = 2816
FF_CHUNK = 256
ROPE_THETA = 10000.0
NORM_EPS = 1e-6
MASK_VALUE = -1e30
GATE_FLOOR = 1e-30
GLA_CHUNK = 64
GLA_SAFE_RANGE = 80.0

TOKEN_TILE = 512
HGRN_TILE = 512
ATTN_TILE = 512
VMEM_LIMIT = 56 * 1024 * 1024


def _dot(a, b):
    return jnp.dot(a, b, preferred_element_type=F32)


def _dot_nt(a, b):
    return lax.dot_general(a, b, (((1,), (1,)), ((), ())), preferred_element_type=F32)


def _dot_tn(a, b):
    return lax.dot_general(a, b, (((0,), (0,)), ((), ())), preferred_element_type=F32)


def _head_blockdiag(n):
    r = lax.broadcasted_iota(jnp.int32, (n, n), 0) >> 6
    c = lax.broadcasted_iota(jnp.int32, (n, n), 1) >> 6
    return (r == c).astype(BF16)


def _head_blockdiag2(n):
    bd = _head_blockdiag(n)
    return jnp.concatenate([bd, bd], axis=0)


def _head_sum(x, bd2):
    hi = x.astype(BF16)
    lo = (x - hi.astype(F32)).astype(BF16)
    return _dot(jnp.concatenate([hi, lo], axis=1), bd2)


def _head_rmsnorm(y, g, bd2):
    return y * lax.rsqrt(_head_sum(y * y, bd2) * (1.0 / HEAD_DIM) + NORM_EPS) * g


def _silu(x):
    return x * jax.nn.sigmoid(x)


def _proj_kernel(x_ref, g1_ref, w_ref, cos_ref, sin_ref, bqg_ref, bkg_ref, cqg_ref, ckg_ref,
                 a_ref, bq_ref, bk_ref, bv_ref,
                 cq1_ref, ck1_ref, cv1_ref, cq4_ref, ck4_ref, cv4_ref, cq16_ref, ck16_ref, cv16_ref,
                 scr_ref):
    tm = x_ref.shape[1]
    x = x_ref[0]
    xn = (x * lax.rsqrt(jnp.mean(x * x, axis=-1, keepdims=True) + NORM_EPS) * g1_ref[...]).astype(BF16)
    a_ref[0] = _dot(xn, w_ref[:, 0:A_COLS])
    y_bc = _dot(xn, w_ref[:, A_COLS:])

    bd = _head_blockdiag2(LANES)
    cos = cos_ref[...]
    sin = sin_ref[...]
    lane = lax.broadcasted_iota(jnp.int32, (1, LANES), 1)
    first_half = (lane & (HEAD_DIM - 1)) < (HEAD_DIM // 2)

    def proj(col):
        return y_bc[:, col - A_COLS:col - A_COLS + LANES]

    def norm_rope(col, g_ref):
        y = _head_rmsnorm(proj(col), g_ref[...], bd)
        partner = jnp.where(first_half, pltpu.roll(y, LANES - HEAD_DIM // 2, 1), pltpu.roll(y, HEAD_DIM // 2, 1))
        return y * cos + partner * sin

    scale = HEAD_DIM ** -0.5
    for j in range(3):
        bq_ref[0, :, j * LANES:(j + 1) * LANES] = (norm_rope(BQ_COL + j * LANES, bqg_ref) * scale).astype(BF16)
    bk = norm_rope(BK_COL, bkg_ref)
    bk_ref[0, :, 0:LANES] = bk.astype(BF16)
    bk_ref[0, :, LANES:2 * LANES] = pltpu.roll(bk, HEAD_DIM, 1).astype(BF16)
    bv = proj(BV_COL)
    bv_ref[0, :, 0:LANES] = bv.astype(BF16)
    bv_ref[0, :, LANES:2 * LANES] = pltpu.roll(bv, HEAD_DIM, 1).astype(BF16)

    def store_dilated(ref, y, d):
        if d == 1:
            ref[0] = y.astype(BF16)
            return
        scr_ref[...] = y
        for r in range(d):
            ref[0, r] = scr_ref[pl.ds(r, tm // d, stride=d), :].astype(BF16)

    c_refs = ((cq1_ref, ck1_ref, cv1_ref), (cq4_ref, ck4_ref, cv4_ref), (cq16_ref, ck16_ref, cv16_ref))
    for p, d in enumerate(C_DILATIONS):
        q_ref, k_ref, v_ref = c_refs[p]
        store_dilated(q_ref, norm_rope(CQ_COL + p * LANES, cqg_ref) * scale, d)
        store_dilated(k_ref, norm_rope(CK_COL + p * LANES, ckg_ref), d)
        store_dilated(v_ref, proj(CV_COL + p * LANES), d)


def _rope_tables(t_len):
    half = HEAD_DIM // 2
    inv = ROPE_THETA ** (-jnp.arange(half, dtype=F32) / half)
    ang = jnp.arange(t_len).astype(F32)[:, None] * inv[None, :]
    cos = jnp.tile(jnp.cos(ang), (1, 2 * LANES // HEAD_DIM))
    sin = jnp.sin(ang)
    sin = jnp.tile(jnp.concatenate([-sin, sin], axis=1), (1, LANES // HEAD_DIM))
    return cos, sin


def _proj_call(x, g1, w_in, cos, sin, bqg, bkg, cqg, ckg):
    n, t, _ = x.shape
    tm = TOKEN_TILE
    grid = (n, t // tm)
    tok = lambda width: pl.BlockSpec((1, tm, width), lambda b, i: (b, i, 0))
    const = lambda shape: pl.BlockSpec(shape, lambda b, i: (0,) * len(shape))
    dil = lambda d: pl.BlockSpec((1, d, tm // d, LANES), lambda b, i: (b, 0, i, 0))
    out_shape = [jax.ShapeDtypeStruct((n, t, A_COLS), F32),
                 jax.ShapeDtypeStruct((n, t, 3 * LANES), BF16),
                 jax.ShapeDtypeStruct((n, t, 2 * LANES), BF16),
                 jax.ShapeDtypeStruct((n, t, 2 * LANES), BF16)]
    out_specs = [tok(A_COLS), tok(3 * LANES), tok(2 * LANES), tok(2 * LANES)]
    for d in C_DILATIONS:
        for _ in range(3):
            if d == 1:
                out_shape.append(jax.ShapeDtypeStruct((n, t, LANES), BF16))
                out_specs.append(tok(LANES))
            else:
                out_shape.append(jax.ShapeDtypeStruct((n, d, t // d, LANES), BF16))
                out_specs.append(dil(d))
    rope_spec = pl.BlockSpec((tm, LANES), lambda b, i: (i, 0))
    return pl.pallas_call(
        _proj_kernel,
        grid=grid,
        in_specs=[tok(D_MODEL), const((1, D_MODEL)), const(w_in.shape), rope_spec, rope_spec,
                  const((1, LANES)), const((1, LANES)), const((1, LANES)), const((1, LANES))],
        out_specs=out_specs,
        out_shape=out_shape,
        scratch_shapes=[pltpu.VMEM((tm, LANES), F32)],
        compiler_params=pltpu.CompilerParams(dimension_semantics=("arbitrary", "arbitrary"),
                                             vmem_limit_bytes=VMEM_LIMIT),
        name="proj",
    )(x, g1, w_in, cos, sin, bqg, bkg, cqg, ckg)


def _hgrn_chunk(q, f_raw, v, lb, state_ref, reverse, exact_refs=None):
    c = GLA_CHUNK
    s = jax.nn.sigmoid(f_raw)
    f = lb + (1.0 - lb) * s
    g = jnp.log(jnp.maximum(f, GATE_FLOOR))
    kk = (1.0 - lb) * (1.0 - s)

    row = lax.broadcasted_iota(jnp.int32, (c, 3 * c), 0)
    col = lax.broadcasted_iota(jnp.int32, (c, 3 * c), 1) & (c - 1)
    tri3 = ((col >= row) if reverse else (col <= row)).astype(BF16)
    g1 = g.astype(BF16)
    r1 = g - g1.astype(F32)
    g2 = r1.astype(BF16)
    g3 = (r1 - g2.astype(F32)).astype(BF16)
    b = _dot(tri3, jnp.concatenate([g1, g2, g3], axis=0))
    total = b[0:1] if reverse else b[c - 1:c]
    mid = b[c // 2:c // 2 + 1] if reverse else b[c // 2 - 1:c // 2]

    lane_head = lax.broadcasted_iota(jnp.int32, (1, A_WIDTH), 1) >> 6
    v16 = v.astype(BF16)

    state = state_ref[...]
    o = _dot_nt((q * jnp.exp(b)).astype(BF16), state.astype(BF16))
    upd = _dot_tn(v16, (kk * jnp.exp(total - b)).astype(BF16))
    srow = lax.broadcasted_iota(jnp.int32, (A_WIDTH, A_WIDTH), 0) >> 6
    scol = lax.broadcasted_iota(jnp.int32, (A_WIDTH, A_WIDTH), 1) >> 6
    state_ref[...] = state * jnp.exp(total) + jnp.where(srow == scol, upd, 0.0)

    offset = jnp.max(jnp.abs(b - mid), axis=0, keepdims=True)

    if exact_refs is None:
        qp = q * jnp.exp(b - mid)
        kp = (kk * jnp.exp(mid - b)).astype(BF16)
        q_stack = jnp.concatenate([jnp.where(lane_head == h, qp, 0.0) for h in range(4)], axis=0).astype(BF16)
        a = _dot_nt(q_stack, kp)
        t_idx = lax.broadcasted_iota(jnp.int32, (4 * c, c), 0) & (c - 1)
        s_idx = lax.broadcasted_iota(jnp.int32, (4 * c, c), 1)
        a = jnp.where((s_idx >= t_idx) if reverse else (s_idx <= t_idx), a, 0.0)
        o2 = _dot(a.astype(BF16), v16)
        for h in range(4):
            o = o + jnp.where(lane_head == h, o2[h * c:(h + 1) * c], 0.0)
        return o, offset

    tb_ref, tk_ref, tv_ref = exact_refs
    tb_ref[...] = b
    tk_ref[...] = kk
    tv_ref[...] = v
    bd = _head_blockdiag(A_WIDTH)
    t_col = lax.broadcasted_iota(jnp.int32, (c, 1), 0)

    def body(si, acc):
        b_s = tb_ref[pl.ds(si, 1), :]
        dec = jnp.exp(jnp.minimum(b - b_s, 0.0))
        valid = (t_col <= si) if reverse else (t_col >= si)
        p = jnp.where(valid, q * dec * tk_ref[pl.ds(si, 1), :], 0.0)
        return acc + _dot(p.astype(BF16), bd) * tv_ref[pl.ds(si, 1), :]

    return lax.fori_loop(0, c, body, o), offset


def _hgrn_kernel(lb_ref, qf_ref, ff_ref, vf_ref, qb_ref, fb_ref, vb_ref, of_ref, ob_ref,
                 sf_ref, sb_ref, sf0_ref, sb0_ref, tb_ref, tk_ref, tv_ref, *, layer, n_chunks):
    @pl.when(pl.program_id(1) == 0)
    def _():
        sf_ref[...] = jnp.zeros_like(sf_ref)
        sb_ref[...] = jnp.zeros_like(sb_ref)

    sf0_ref[...] = sf_ref[...]
    sb0_ref[...] = sb_ref[...]

    raw = lb_ref[...]
    depth = raw.shape[0]
    mx = raw[0:1]
    for i in range(1, depth):
        mx = jnp.maximum(mx, raw[i:i + 1])
    e = jnp.exp(raw - mx)
    den = e[0:1]
    for i in range(1, depth):
        den = den + e[i:i + 1]
    sm = e / den
    cs = sm[0:1]
    for i in range(1, layer + 1):
        cs = cs + sm[i:i + 1]
    lb = cs - sm[0:1]
    lb_f = lb[:, 0:A_WIDTH]
    lb_b = lb[:, A_WIDTH:2 * A_WIDTH]

    def chunk_rows(ci):
        start = ci * GLA_CHUNK
        return pl.ds(start if isinstance(ci, int) else pl.multiple_of(start, GLA_CHUNK), GLA_CHUNK)

    def run_chunk(ci, exact_refs):
        rows = chunk_rows(ci)
        o_f, off_f = _hgrn_chunk(qf_ref[0, rows, :], ff_ref[0, rows, :], vf_ref[0, rows, :], lb_f, sf_ref,
                                 reverse=False, exact_refs=exact_refs)
        of_ref[0, rows, :] = o_f
        rows = chunk_rows(n_chunks - 1 - ci)
        o_b, off_b = _hgrn_chunk(qb_ref[0, rows, :], fb_ref[0, rows, :], vb_ref[0, rows, :], lb_b, sb_ref,
                                 reverse=True, exact_refs=exact_refs)
        ob_ref[0, rows, :] = o_b
        return jnp.maximum(off_f, off_b)

    worst = jnp.zeros((1, A_WIDTH), F32)
    for ci in range(n_chunks):
        worst = jnp.maximum(worst, run_chunk(ci, None))

    @pl.when(jnp.max(worst) > GLA_SAFE_RANGE)
    def _():
        sf_ref[...] = sf0_ref[...]
        sb_ref[...] = sb0_ref[...]

        def step(ci, carry):
            run_chunk(ci, (tb_ref, tk_ref, tv_ref))
            return carry

        lax.fori_loop(0, n_chunks, step, 0)


def _hgrn_call(a_raw, lb_raw2, layer):
    n, t, _ = a_raw.shape
    tt = HGRN_TILE
    nt = t // tt
    fwd = lambda cb: pl.BlockSpec((1, tt, A_WIDTH), lambda b, i: (b, i, cb))
    bwd = lambda cb: pl.BlockSpec((1, tt, A_WIDTH), lambda b, i: (b, nt - 1 - i, cb))
    return pl.pallas_call(
        functools.partial(_hgrn_kernel, layer=layer, n_chunks=tt // GLA_CHUNK),
        grid=(n, nt),
        in_specs=[pl.BlockSpec(lb_raw2.shape, lambda b, i: (0, 0)),
                  fwd(0), fwd(1), fwd(3), bwd(0), bwd(2), bwd(3)],
        out_specs=[fwd(0), bwd(0)],
        out_shape=[jax.ShapeDtypeStruct((n, t, A_WIDTH), F32)] * 2,
        scratch_shapes=[pltpu.VMEM((A_WIDTH, A_WIDTH), F32)] * 4 + [pltpu.VMEM((GLA_CHUNK, A_WIDTH), F32)] * 3,
        compiler_params=pltpu.CompilerParams(dimension_semantics=("arbitrary", "arbitrary"),
                                             vmem_limit_bytes=VMEM_LIMIT),
        name="hgrn",
    )(lb_raw2, a_raw, a_raw, a_raw, a_raw, a_raw, a_raw)


def _attn_kernel(*refs, w, seq_len, heads, use_sink, want_lse):
    q_ref, kp_ref, kc_ref, kn_ref, vp_ref, vc_ref, vn_ref = refs[:7]
    rest = list(refs[7:])
    sink_ref = rest.pop(0) if use_sink else None
    o_ref = rest.pop(0)
    lse_ref = rest.pop(0) if want_lse else None

    tr = q_ref.shape[1]
    sb_rows, sb_keys = 2 * w, 4 * w
    t0 = pl.program_id(1) * tr
    k_ext = jnp.concatenate([kp_ref[0], kc_ref[0], kn_ref[0]], axis=0)
    v_ext = jnp.concatenate([vp_ref[0], vc_ref[0], vn_ref[0]], axis=0)
    lane_half = lax.broadcasted_iota(jnp.int32, (1, LANES), 1) >> 6

    kv = {}
    for blk_heads in heads:
        for half, kv_blk in blk_heads:
            if (half, kv_blk) not in kv:
                sel = lane_half == half
                kv[(half, kv_blk)] = (jnp.where(sel, k_ext[:, kv_blk * LANES:(kv_blk + 1) * LANES], 0),
                                      jnp.where(sel, v_ext[:, kv_blk * LANES:(kv_blk + 1) * LANES], 0))

    row = lax.broadcasted_iota(jnp.int32, (sb_rows, sb_keys), 0)
    col = lax.broadcasted_iota(jnp.int32, (sb_rows, sb_keys), 1)
    band = (col >= row) & (col - row <= 2 * w)
    for sb in range(tr // sb_rows):
        r0 = sb * sb_rows
        key_pos = col + (t0 - w + r0)
        mask = band & (key_pos >= 0) & (key_pos < seq_len)
        for j, blk_heads in enumerate(heads):
            q = q_ref[0, r0:r0 + sb_rows, j * LANES:(j + 1) * LANES]
            out = None
            lse_out = None
            for e, (half, kv_blk) in enumerate(blk_heads):
                k_h, v_h = kv[(half, kv_blk)]
                s = _dot_nt(q, k_h[r0:r0 + sb_keys])
                s = jnp.where(mask, s, MASK_VALUE)
                m = jnp.max(s, axis=-1, keepdims=True)
                if use_sink:
                    sink = sink_ref[2 * j + e]
                    m = jnp.maximum(m, sink)
                p = jnp.exp(s - m)
                den = jnp.sum(p, axis=-1, keepdims=True)
                if use_sink:
                    den = den + jnp.exp(sink - m)
                o = _dot(p.astype(BF16), v_h[r0:r0 + sb_keys]) / den
                out = o if out is None else out + o
                if want_lse:
                    lse = m + jnp.log(den)
                    lse_out = lse if lse_out is None else jnp.where(lane_half == half, lse, lse_out)
            o_ref[0, r0:r0 + sb_rows, j * LANES:(j + 1) * LANES] = out.astype(o_ref.dtype)
            if want_lse:
                lse_ref[0, r0:r0 + sb_rows, :] = jnp.broadcast_to(lse_out, (sb_rows, LANES))


def _attn_call(q, k, v, sink, *, w, heads, want_lse, name):
    n, seq_len, qw = q.shape
    kw = k.shape[-1]
    tr = min(ATTN_TILE, seq_len)
    per = tr // w
    last = seq_len // w - 1
    cur = lambda width: pl.BlockSpec((1, tr, width), lambda b, i: (b, i, 0))
    prev = pl.BlockSpec((1, w, kw), lambda b, i: (b, jnp.maximum(i * per - 1, 0), 0))
    nxt = pl.BlockSpec((1, w, kw), lambda b, i: (b, jnp.minimum((i + 1) * per, last), 0))
    in_specs = [cur(qw), prev, cur(kw), nxt, prev, cur(kw), nxt]
    args = [q, k, k, k, v, v, v]
    if sink is not None:
        in_specs.append(pl.BlockSpec(memory_space=pltpu.SMEM))
        args.append(sink)
    out_shape = [jax.ShapeDtypeStruct((n, seq_len, qw), BF16)]
    out_specs = [cur(qw)]
    if want_lse:
        out_shape.append(jax.ShapeDtypeStruct((n, seq_len, LANES), F32))
        out_specs.append(cur(LANES))
    return pl.pallas_call(
        functools.partial(_attn_kernel, w=w, seq_len=seq_len, heads=heads, use_sink=sink is not None,
                          want_lse=want_lse),
        grid=(n, seq_len // tr),
        in_specs=in_specs,
        out_specs=out_specs,
        out_shape=out_shape,
        compiler_params=pltpu.CompilerParams(dimension_semantics=("arbitrary", "arbitrary"),
                                             vmem_limit_bytes=VMEM_LIMIT),
        name=name,
    )(*args)


_B_HEADS = tuple(tuple((h % 2, 0 if (h // 3) == (h % 2) else 1) for h in (2 * j, 2 * j + 1)) for j in range(3))
_C_HEADS = (((0, 0), (1, 0)),)


def _outffn_kernel(x_ref, of_ref, ob_ref, gate_ref, ag_ref, yb_ref,
                   o1_ref, l1_ref, o4_ref, l4_ref, o16_ref, l16_ref,
                   wout_ref, n2_ref, wg_ref, wu_ref, wd_ref,
                   y_ref, s4o_ref, s4l_ref, s16o_ref, s16l_ref, hn_ref, acc_ref):
    tm = x_ref.shape[1]
    bd = _head_blockdiag2(LANES)
    ya = []
    for j in range(A_WIDTH // LANES):
        cols = slice(j * LANES, (j + 1) * LANES)
        o = of_ref[0, :, cols] + ob_ref[0, :, cols]
        ya.append(_head_rmsnorm(o, ag_ref[...], bd) * _silu(gate_ref[0, :, cols]))
    ya = jnp.concatenate(ya, axis=1).astype(BF16)

    def interleave(o_ref, l_ref, so_ref, sl_ref, d):
        for r in range(d):
            so_ref[pl.ds(r, tm // d, stride=d), :] = o_ref[0, r].astype(F32)
            sl_ref[pl.ds(r, tm // d, stride=d), :] = l_ref[0, r]
        return so_ref[...], sl_ref[...]

    o1, l1 = o1_ref[0].astype(F32), l1_ref[0]
    o4, l4 = interleave(o4_ref, l4_ref, s4o_ref, s4l_ref, 4)
    o16, l16 = interleave(o16_ref, l16_ref, s16o_ref, s16l_ref, 16)
    m = jnp.maximum(jnp.maximum(l1, l4), l16)
    e1, e4, e16 = jnp.exp(l1 - m), jnp.exp(l4 - m), jnp.exp(l16 - m)
    yc = ((e1 * o1 + e4 * o4 + e16 * o16) / (e1 + e4 + e16)).astype(BF16)

    h = (x_ref[0] + _dot(ya, wout_ref[0:A_WIDTH]) + _dot(yb_ref[0], wout_ref[A_WIDTH:A_WIDTH + 3 * LANES])
         + _dot(yc, wout_ref[A_WIDTH + 3 * LANES:A_WIDTH + 4 * LANES]))
    hn_ref[...] = (h * lax.rsqrt(jnp.mean(h * h, axis=-1, keepdims=True) + NORM_EPS) * n2_ref[...]).astype(BF16)
    acc_ref[...] = h

    def ff(ci, carry):
        hn = hn_ref[...]
        act = (_silu(_dot(hn, wg_ref[ci])) * _dot(hn, wu_ref[ci])).astype(BF16)
        acc_ref[...] += _dot(act, wd_ref[ci])
        return carry

    lax.fori_loop(0, D_FF // FF_CHUNK, ff, 0)
    y_ref[0] = acc_ref[...]


def _outffn_call(x, o_f, o_b, a_raw, ag, yb, c_out, w_out, n2, wg, wu, wd):
    n, t, _ = x.shape
    tm = TOKEN_TILE
    tok = lambda width, cb=0: pl.BlockSpec((1, tm, width), lambda b, i: (b, i, cb))
    dil = lambda d: pl.BlockSpec((1, d, tm // d, LANES), lambda b, i: (b, 0, i, 0))
    const = lambda shape: pl.BlockSpec(shape, lambda b, i: (0,) * len(shape), pipeline_mode=pl.Buffered(1))
    (o1, l1), (o4, l4), (o16, l16) = c_out
    return pl.pallas_call(
        _outffn_kernel,
        grid=(n, t // tm),
        in_specs=[tok(D_MODEL), tok(A_WIDTH), tok(A_WIDTH), tok(A_WIDTH, 4), const((1, LANES)), tok(3 * LANES),
                  tok(LANES), tok(LANES), dil(4), dil(4), dil(16), dil(16),
                  const(w_out.shape), const((1, D_MODEL)), const(wg.shape), const(wu.shape), const(wd.shape)],
        out_specs=tok(D_MODEL),
        out_shape=jax.ShapeDtypeStruct((n, t, D_MODEL), F32),
        scratch_shapes=[pltpu.VMEM((tm, LANES), F32)] * 4
        + [pltpu.VMEM((tm, D_MODEL), BF16), pltpu.VMEM((tm, D_MODEL), F32)],
        compiler_params=pltpu.CompilerParams(dimension_semantics=("arbitrary", "arbitrary"),
                                             vmem_limit_bytes=VMEM_LIMIT),
        name="outffn",
    )(x, o_f, o_b, a_raw, ag, yb, o1, l1, o4, l4, o16, l16, w_out, n2, wg, wu, wd)


def _layer(x, p, layer, cos, sin):
    n, t, _ = x.shape
    outs = _proj_call(x, p["n1"], p["w_in"], cos, sin, p["bqg"], p["bkg"], p["cqg"], p["ckg"])
    a_raw, bq, bk, bv = outs[:4]
    o_f, o_b = _hgrn_call(a_raw, p["lb_raw"], layer)
    (yb,) = _attn_call(bq, bk, bv, p["sink"], w=B_HALF_WINDOW, heads=_B_HEADS, want_lse=False, name="attn_window")
    c_out = []
    for i, d in enumerate(C_DILATIONS):
        q, k, v = (a.reshape(n * d, t // d, LANES) for a in outs[4 + 3 * i:7 + 3 * i])
        o, lse = _attn_call(q, k, v, None, w=C_HALF_WINDOW, heads=_C_HEADS, want_lse=True, name=f"attn_dil{d}")
        shape = (n, t, LANES) if d == 1 else (n, d, t // d, LANES)
        c_out.append((o.reshape(shape), lse.reshape(shape)))
    return _outffn_call(x, o_f, o_b, a_raw, p["ag"], yb, c_out, p["w_out"], p["n2"], p["wg"], p["wu"], p["wd"])


def _layer_params(l, norm1_g, w_in, lb_raw, a_norm_g, b_qn_g, b_kn_g, b_sink, c_qn_g, c_kn_g, w_out, norm2_g,
                  w_gate, w_up, w_down):
    pair = lambda g: jnp.tile(g[l].astype(F32), LANES // HEAD_DIM)[None, :]
    chunks = D_FF // FF_CHUNK
    return {
        "n1": norm1_g[l].astype(F32)[None, :],
        "w_in": w_in[l].astype(BF16),
        "lb_raw": lb_raw.astype(F32).reshape(lb_raw.shape[0], 2 * A_WIDTH),
        "ag": pair(a_norm_g), "bqg": pair(b_qn_g), "bkg": pair(b_kn_g), "cqg": pair(c_qn_g), "ckg": pair(c_kn_g),
        "sink": b_sink[l].astype(F32),
        "w_out": w_out[l].astype(BF16),
        "n2": norm2_g[l].astype(F32)[None, :],
        "wg": w_gate[l].astype(BF16).reshape(D_MODEL, chunks, FF_CHUNK).transpose(1, 0, 2),
        "wu": w_up[l].astype(BF16).reshape(D_MODEL, chunks, FF_CHUNK).transpose(1, 0, 2),
        "wd": w_down[l].astype(BF16).reshape(chunks, FF_CHUNK, D_MODEL),
    }


def kernel(x_prompt, x_sample, norm1_g, w_in, lb_raw, a_norm_g, b_qn_g, b_kn_g, b_sink, c_qn_g, c_kn_g, w_out,
           norm2_g, w_gate, w_up, w_down):
    depth = w_in.shape[0]
    params = [_layer_params(l, norm1_g, w_in, lb_raw, a_norm_g, b_qn_g, b_kn_g, b_sink, c_qn_g, c_kn_g, w_out,
                            norm2_g, w_gate, w_up, w_down) for l in range(depth)]

    def trunk(x):
        cos, sin = _rope_tables(x.shape[1])
        for l in range(depth):
            x = _layer(x, params[l], l, cos, sin)
        return x

    return (trunk(x_prompt), trunk(x_sample))
```

```python
import functools

import jax
import jax.numpy as jnp
import numpy as np
from jax import lax
from jax.experimental import pallas as pl
from jax.experimental.pallas import tpu as pltpu

F32 = jnp.float32
BF16 = jnp.bfloat16

D_MODEL = 1024
HEAD_DIM = 64
LANES = 128
A_WIDTH = 256
A_COLS = 5 * A_WIDTH
BQ_COL, BK_COL, BV_COL = 1280, 1664, 1792
CQ_COL, CK_COL, CV_COL = 1920, 2304, 2688
B_HALF_WINDOW = 128
C_HALF_WINDOW = 64
C_DILATIONS = (1, 4, 16)
D_FF = 2816
FF_GROUPS = (1024, 1024, 768)
ROPE_THETA = 10000.0
NORM_EPS = 1e-6
MASK_VALUE = -1e30
GATE_FLOOR = 1e-30
GLA_CHUNK = 64
GLA_SAFE_RANGE = 80.0

TOKEN_TILE = 512
HGRN_TILE = 512
ATTN_TILE = 512
VMEM_LIMIT = 56 * 1024 * 1024


def _dot(a, b):
    return jnp.dot(a, b, preferred_element_type=F32)


def _dot_nt(a, b):
    return lax.dot_general(a, b, (((1,), (1,)), ((), ())), preferred_element_type=F32)


def _dot_tn(a, b):
    return lax.dot_general(a, b, (((0,), (0,)), ((), ())), preferred_element_type=F32)


def _head_blockdiag(n):
    r = lax.broadcasted_iota(jnp.int32, (n, n), 0) >> 6
    c = lax.broadcasted_iota(jnp.int32, (n, n), 1) >> 6
    return (r == c).astype(BF16)


def _head_blockdiag2(n):
    bd = _head_blockdiag(n)
    return jnp.concatenate([bd, bd], axis=0)


def _head_sum(x, bd2):
    hi = x.astype(BF16)
    lo = (x - hi.astype(F32)).astype(BF16)
    return _dot(jnp.concatenate([hi, lo], axis=1), bd2)


def _head_rmsnorm(y, g, bd2):
    return y * lax.rsqrt(_head_sum(y * y, bd2) * (1.0 / HEAD_DIM) + NORM_EPS) * g


def _silu(x):
    return x * jax.nn.sigmoid(x)


def _proj_kernel(x_ref, g1_ref, w_ref, cos_ref, sin_ref, bqg_ref, bkg_ref, cqg_ref, ckg_ref,
                 a_ref, bq_ref, bk_ref, bv_ref,
                 cq1_ref, ck1_ref, cv1_ref, cq4_ref, ck4_ref, cv4_ref, cq16_ref, ck16_ref, cv16_ref,
                 scr_ref):
    tm = x_ref.shape[1]
    x = x_ref[0]
    xn = (x * lax.rsqrt(jnp.mean(x * x, axis=-1, keepdims=True) + NORM_EPS) * g1_ref[...]).astype(BF16)
    a_ref[0] = _dot(xn, w_ref[:, 0:A_COLS])
    y_bc = _dot(xn, w_ref[:, A_COLS:])

    bd = _head_blockdiag2(LANES)
    cos = cos_ref[...]
    sin = sin_ref[...]
    lane = lax.broadcasted_iota(jnp.int32, (1, LANES), 1)
    first_half = (lane & (HEAD_DIM - 1)) < (HEAD_DIM // 2)

    def proj(col):
        return y_bc[:, col - A_COLS:col - A_COLS + LANES]

    def norm_rope(col, g_ref):
        y = _head_rmsnorm(proj(col), g_ref[...], bd)
        partner = jnp.where(first_half, pltpu.roll(y, LANES - HEAD_DIM // 2, 1), pltpu.roll(y, HEAD_DIM // 2, 1))
        return y * cos + partner * sin

    scale = HEAD_DIM ** -0.5
    for j in range(3):
        bq_ref[0, :, j * LANES:(j + 1) * LANES] = (norm_rope(BQ_COL + j * LANES, bqg_ref) * scale).astype(BF16)
    bk = norm_rope(BK_COL, bkg_ref)
    bk_ref[0, :, 0:LANES] = bk.astype(BF16)
    bk_ref[0, :, LANES:2 * LANES] = pltpu.roll(bk, HEAD_DIM, 1).astype(BF16)
    bv = proj(BV_COL)
    bv_ref[0, :, 0:LANES] = bv.astype(BF16)
    bv_ref[0, :, LANES:2 * LANES] = pltpu.roll(bv, HEAD_DIM, 1).astype(BF16)

    def store_dilated(ref, y, d):
        if d == 1:
            ref[0] = y.astype(BF16)
            return
        scr_ref[...] = y
        for r in range(d):
            ref[0, r] = scr_ref[pl.ds(r, tm // d, stride=d), :].astype(BF16)

    c_refs = ((cq1_ref, ck1_ref, cv1_ref), (cq4_ref, ck4_ref, cv4_ref), (cq16_ref, ck16_ref, cv16_ref))
    for p, d in enumerate(C_DILATIONS):
        q_ref, k_ref, v_ref = c_refs[p]
        store_dilated(q_ref, norm_rope(CQ_COL + p * LANES, cqg_ref) * scale, d)
        store_dilated(k_ref, norm_rope(CK_COL + p * LANES, ckg_ref), d)
        store_dilated(v_ref, proj(CV_COL + p * LANES), d)


def _rope_tables(t_len):
    half = HEAD_DIM // 2
    inv = ROPE_THETA ** (-jnp.arange(half, dtype=F32) / half)
    ang = jnp.arange(t_len).astype(F32)[:, None] * inv[None, :]
    cos = jnp.tile(jnp.cos(ang), (1, 2 * LANES // HEAD_DIM))
    sin = jnp.sin(ang)
    sin = jnp.tile(jnp.concatenate([-sin, sin], axis=1), (1, LANES // HEAD_DIM))
    return cos, sin


def _proj_call(x, g1, w_in, cos, sin, bqg, bkg, cqg, ckg):
    n, t, _ = x.shape
    tm = TOKEN_TILE
    grid = (n, t // tm)
    tok = lambda width: pl.BlockSpec((1, tm, width), lambda b, i: (b, i, 0))
    const = lambda shape: pl.BlockSpec(shape, lambda b, i: (0,) * len(shape))
    dil = lambda d: pl.BlockSpec((1, d, tm // d, LANES), lambda b, i: (b, 0, i, 0))
    out_shape = [jax.ShapeDtypeStruct((n, t, A_COLS), F32),
                 jax.ShapeDtypeStruct((n, t, 3 * LANES), BF16),
                 jax.ShapeDtypeStruct((n, t, 2 * LANES), BF16),
                 jax.ShapeDtypeStruct((n, t, 2 * LANES), BF16)]
    out_specs = [tok(A_COLS), tok(3 * LANES), tok(2 * LANES), tok(2 * LANES)]
    for d in C_DILATIONS:
        for _ in range(3):
            if d == 1:
                out_shape.append(jax.ShapeDtypeStruct((n, t, LANES), BF16))
                out_specs.append(tok(LANES))
            else:
                out_shape.append(jax.ShapeDtypeStruct((n, d, t // d, LANES), BF16))
                out_specs.append(dil(d))
    rope_spec = pl.BlockSpec((tm, LANES), lambda b, i: (i, 0))
    return pl.pallas_call(
        _proj_kernel,
        grid=grid,
        in_specs=[tok(D_MODEL), const((1, D_MODEL)), const(w_in.shape), rope_spec, rope_spec,
                  const((1, LANES)), const((1, LANES)), const((1, LANES)), const((1, LANES))],
        out_specs=out_specs,
        out_shape=out_shape,
        scratch_shapes=[pltpu.VMEM((tm, LANES), F32)],
        compiler_params=pltpu.CompilerParams(dimension_semantics=("arbitrary", "arbitrary"),
                                             vmem_limit_bytes=VMEM_LIMIT),
        name="proj",
    )(x, g1, w_in, cos, sin, bqg, bkg, cqg, ckg)


def _hgrn_chunk(q, f_raw, v, lb, state_ref, reverse, exact_refs=None):
    c = GLA_CHUNK
    s = jax.nn.sigmoid(f_raw)
    f = lb + (1.0 - lb) * s
    g = jnp.log(jnp.maximum(f, GATE_FLOOR))
    kk = (1.0 - lb) * (1.0 - s)

    row = lax.broadcasted_iota(jnp.int32, (c, 3 * c), 0)
    col = lax.broadcasted_iota(jnp.int32, (c, 3 * c), 1) & (c - 1)
    tri3 = ((col >= row) if reverse else (col <= row)).astype(BF16)
    g1 = g.astype(BF16)
    r1 = g - g1.astype(F32)
    g2 = r1.astype(BF16)
    g3 = (r1 - g2.astype(F32)).astype(BF16)
    b = _dot(tri3, jnp.concatenate([g1, g2, g3], axis=0))
    total = b[0:1] if reverse else b[c - 1:c]
    mid = b[c // 2:c // 2 + 1] if reverse else b[c // 2 - 1:c // 2]

    lane_head = lax.broadcasted_iota(jnp.int32, (1, A_WIDTH), 1) >> 6
    v16 = v.astype(BF16)

    state = state_ref[...]
    o = _dot_nt((q * jnp.exp(b)).astype(BF16), state.astype(BF16))
    upd = _dot_tn(v16, (kk * jnp.exp(total - b)).astype(BF16))
    srow = lax.broadcasted_iota(jnp.int32, (A_WIDTH, A_WIDTH), 0) >> 6
    scol = lax.broadcasted_iota(jnp.int32, (A_WIDTH, A_WIDTH), 1) >> 6
    state_ref[...] = state * jnp.exp(total) + jnp.where(srow == scol, upd, 0.0)

    offset = jnp.max(jnp.abs(b - mid), axis=0, keepdims=True)

    if exact_refs is None:
        qp = q * jnp.exp(b - mid)
        kp = (kk * jnp.exp(mid - b)).astype(BF16)
        q_stack = jnp.concatenate([jnp.where(lane_head == h, qp, 0.0) for h in range(4)], axis=0).astype(BF16)
        a = _dot_nt(q_stack, kp)
        t_idx = lax.broadcasted_iota(jnp.int32, (4 * c, c), 0) & (c - 1)
        s_idx = lax.broadcasted_iota(jnp.int32, (4 * c, c), 1)
        a = jnp.where((s_idx >= t_idx) if reverse else (s_idx <= t_idx), a, 0.0)
        o2 = _dot(a.astype(BF16), v16)
        for h in range(4):
            o = o + jnp.where(lane_head == h, o2[h * c:(h + 1) * c], 0.0)
        return o, offset

    tb_ref, tk_ref, tv_ref = exact_refs
    tb_ref[...] = b
    tk_ref[...] = kk
    tv_ref[...] = v
    bd = _head_blockdiag(A_WIDTH)
    t_col = lax.broadcasted_iota(jnp.int32, (c, 1), 0)

    def body(si, acc):
        b_s = tb_ref[pl.ds(si, 1), :]
        dec = jnp.exp(jnp.minimum(b - b_s, 0.0))
        valid = (t_col <= si) if reverse else (t_col >= si)
        p = jnp.where(valid, q * dec * tk_ref[pl.ds(si, 1), :], 0.0)
        return acc + _dot(p.astype(BF16), bd) * tv_ref[pl.ds(si, 1), :]

    return lax.fori_loop(0, c, body, o), offset


def _hgrn_kernel(lb_ref, qf_ref, ff_ref, vf_ref, qb_ref, fb_ref, vb_ref, of_ref, ob_ref,
                 sf_ref, sb_ref, sf0_ref, sb0_ref, tb_ref, tk_ref, tv_ref, *, layer, n_chunks):
    @pl.when(pl.program_id(1) == 0)
    def _():
        sf_ref[...] = jnp.zeros_like(sf_ref)
        sb_ref[...] = jnp.zeros_like(sb_ref)

    sf0_ref[...] = sf_ref[...]
    sb0_ref[...] = sb_ref[...]

    raw = lb_ref[...]
    depth = raw.shape[0]
    mx = raw[0:1]
    for i in range(1, depth):
        mx = jnp.maximum(mx, raw[i:i + 1])
    e = jnp.exp(raw - mx)
    den = e[0:1]
    for i in range(1, depth):
        den = den + e[i:i + 1]
    sm = e / den
    cs = sm[0:1]
    for i in range(1, layer + 1):
        cs = cs + sm[i:i + 1]
    lb = cs - sm[0:1]
    lb_f = lb[:, 0:A_WIDTH]
    lb_b = lb[:, A_WIDTH:2 * A_WIDTH]

    def chunk_rows(ci):
        start = ci * GLA_CHUNK
        return pl.ds(start if isinstance(ci, int) else pl.multiple_of(start, GLA_CHUNK), GLA_CHUNK)

    def run_chunk(ci, exact_refs):
        rows = chunk_rows(ci)
        o_f, off_f = _hgrn_chunk(qf_ref[0, rows, :], ff_ref[0, rows, :], vf_ref[0, rows, :], lb_f, sf_ref,
                                 reverse=False, exact_refs=exact_refs)
        of_ref[0, rows, :] = o_f
        rows = chunk_rows(n_chunks - 1 - ci)
        o_b, off_b = _hgrn_chunk(qb_ref[0, rows, :], fb_ref[0, rows, :], vb_ref[0, rows, :], lb_b, sb_ref,
                                 reverse=True, exact_refs=exact_refs)
        ob_ref[0, rows, :] = o_b
        return jnp.maximum(off_f, off_b)

    worst = jnp.zeros((1, A_WIDTH), F32)
    for ci in range(n_chunks):
        worst = jnp.maximum(worst, run_chunk(ci, None))

    @pl.when(jnp.max(worst) > GLA_SAFE_RANGE)
    def _():
        sf_ref[...] = sf0_ref[...]
        sb_ref[...] = sb0_ref[...]

        def step(ci, carry):
            run_chunk(ci, (tb_ref, tk_ref, tv_ref))
            return carry

        lax.fori_loop(0, n_chunks, step, 0)


def _hgrn_call(a_raw, lb_raw2, layer):
    n, t, _ = a_raw.shape
    tt = HGRN_TILE
    nt = t // tt
    fwd = lambda cb: pl.BlockSpec((1, tt, A_WIDTH), lambda b, i: (b, i, cb))
    bwd = lambda cb: pl.BlockSpec((1, tt, A_WIDTH), lambda b, i: (b, nt - 1 - i, cb))
    return pl.pallas_call(
        functools.partial(_hgrn_kernel, layer=layer, n_chunks=tt // GLA_CHUNK),
        grid=(n, nt),
        in_specs=[pl.BlockSpec(lb_raw2.shape, lambda b, i: (0, 0)),
                  fwd(0), fwd(1), fwd(3), bwd(0), bwd(2), bwd(3)],
        out_specs=[fwd(0), bwd(0)],
        out_shape=[jax.ShapeDtypeStruct((n, t, A_WIDTH), F32)] * 2,
        scratch_shapes=[pltpu.VMEM((A_WIDTH, A_WIDTH), F32)] * 4 + [pltpu.VMEM((GLA_CHUNK, A_WIDTH), F32)] * 3,
        compiler_params=pltpu.CompilerParams(dimension_semantics=("arbitrary", "arbitrary"),
                                             vmem_limit_bytes=VMEM_LIMIT),
        name="hgrn",
    )(lb_raw2, a_raw, a_raw, a_raw, a_raw, a_raw, a_raw)


def _attn_kernel(*refs, w, seq_len, heads, use_sink, want_lse):
    q_ref, kp_ref, kc_ref, kn_ref, vp_ref, vc_ref, vn_ref = refs[:7]
    rest = list(refs[7:])
    sink_ref = rest.pop(0) if use_sink else None
    o_ref = rest.pop(0)
    lse_ref = rest.pop(0) if want_lse else None

    tr = q_ref.shape[1]
    sb_rows, sb_keys = 2 * w, 4 * w
    t0 = pl.program_id(1) * tr
    k_ext = jnp.concatenate([kp_ref[0], kc_ref[0], kn_ref[0]], axis=0)
    v_ext = jnp.concatenate([vp_ref[0], vc_ref[0], vn_ref[0]], axis=0)
    lane_half = lax.broadcasted_iota(jnp.int32, (1, LANES), 1) >> 6

    kv = {}
    for blk_heads in heads:
        for half, kv_blk in blk_heads:
            if (half, kv_blk) not in kv:
                sel = lane_half == half
                kv[(half, kv_blk)] = (jnp.where(sel, k_ext[:, kv_blk * LANES:(kv_blk + 1) * LANES], 0),
                                      jnp.where(sel, v_ext[:, kv_blk * LANES:(kv_blk + 1) * LANES], 0))

    row = lax.broadcasted_iota(jnp.int32, (sb_rows, sb_keys), 0)
    col = lax.broadcasted_iota(jnp.int32, (sb_rows, sb_keys), 1)
    band = (col >= row) & (col - row <= 2 * w)
    for sb in range(tr // sb_rows):
        r0 = sb * sb_rows
        key_pos = col + (t0 - w + r0)
        mask = band & (key_pos >= 0) & (key_pos < seq_len)
        for j, blk_heads in enumerate(heads):
            q = q_ref[0, r0:r0 + sb_rows, j * LANES:(j + 1) * LANES]
            out = None
            lse_out = None
            for e, (half, kv_blk) in enumerate(blk_heads):
                k_h, v_h = kv[(half, kv_blk)]
                s = _dot_nt(q, k_h[r0:r0 + sb_keys])
                s = jnp.where(mask, s, MASK_VALUE)
                m = jnp.max(s, axis=-1, keepdims=True)
                if use_sink:
                    sink = sink_ref[2 * j + e]
                    m = jnp.maximum(m, sink)
                p = jnp.exp(s - m)
                den = jnp.sum(p, axis=-1, keepdims=True)
                if use_sink:
                    den = den + jnp.exp(sink - m)
                o = _dot(p.astype(BF16), v_h[r0:r0 + sb_keys]) / den
                out = o if out is None else out + o
                if want_lse:
                    lse = m + jnp.log(den)
                    lse_out = lse if lse_out is None else jnp.where(lane_half == half, lse, lse_out)
            o_ref[0, r0:r0 + sb_rows, j * LANES:(j + 1) * LANES] = out.astype(o_ref.dtype)
            if want_lse:
                lse_ref[0, r0:r0 + sb_rows, :] = jnp.broadcast_to(lse_out, (sb_rows, LANES))


def _attn_call(q, k, v, sink, *, w, heads, want_lse, name):
    n, seq_len, qw = q.shape
    kw = k.shape[-1]
    tr = min(ATTN_TILE, seq_len)
    per = tr // w
    last = seq_len // w - 1
    cur = lambda width: pl.BlockSpec((1, tr, width), lambda b, i: (b, i, 0))
    prev = pl.BlockSpec((1, w, kw), lambda b, i: (b, jnp.maximum(i * per - 1, 0), 0))
    nxt = pl.BlockSpec((1, w, kw), lambda b, i: (b, jnp.minimum((i + 1) * per, last), 0))
    in_specs = [cur(qw), prev, cur(kw), nxt, prev, cur(kw), nxt]
    args = [q, k, k, k, v, v, v]
    if sink is not None:
        in_specs.append(pl.BlockSpec(memory_space=pltpu.SMEM))
        args.append(sink)
    out_shape = [jax.ShapeDtypeStruct((n, seq_len, qw), BF16)]
    out_specs = [cur(qw)]
    if want_lse:
        out_shape.append(jax.ShapeDtypeStruct((n, seq_len, LANES), F32))
        out_specs.append(cur(LANES))
    return pl.pallas_call(
        functools.partial(_attn_kernel, w=w, seq_len=seq_len, heads=heads, use_sink=sink is not None,
                          want_lse=want_lse),
        grid=(n, seq_len // tr),
        in_specs=in_specs,
        out_specs=out_specs,
        out_shape=out_shape,
        compiler_params=pltpu.CompilerParams(dimension_semantics=("arbitrary", "arbitrary"),
                                             vmem_limit_bytes=VMEM_LIMIT),
        name=name,
    )(*args)


_B_HEADS = tuple(tuple((h % 2, 0 if (h // 3) == (h % 2) else 1) for h in (2 * j, 2 * j + 1)) for j in range(3))
_C_HEADS = (((0, 0), (1, 0)),)


def _outffn_kernel(x_ref, of_ref, ob_ref, gate_ref, ag_ref, yb_ref,
                   o1_ref, l1_ref, o4_ref, l4_ref, o16_ref, l16_ref,
                   wout_ref, n2_ref, wg_ref, wu_ref, wd_ref,
                   y_ref, s4o_ref, s4l_ref, s16o_ref, s16l_ref):
    tm = x_ref.shape[1]
    bd = _head_blockdiag2(LANES)
    ya = []
    for j in range(A_WIDTH // LANES):
        cols = slice(j * LANES, (j + 1) * LANES)
        o = of_ref[0, :, cols] + ob_ref[0, :, cols]
        ya.append(_head_rmsnorm(o, ag_ref[...], bd) * _silu(gate_ref[0, :, cols]))
    ya = jnp.concatenate(ya, axis=1).astype(BF16)

    def interleave(o_ref, l_ref, so_ref, sl_ref, d):
        for r in range(d):
            so_ref[pl.ds(r, tm // d, stride=d), :] = o_ref[0, r].astype(F32)
            sl_ref[pl.ds(r, tm // d, stride=d), :] = l_ref[0, r]
        return so_ref[...], sl_ref[...]

    o1, l1 = o1_ref[0].astype(F32), l1_ref[0]
    o4, l4 = interleave(o4_ref, l4_ref, s4o_ref, s4l_ref, 4)
    o16, l16 = interleave(o16_ref, l16_ref, s16o_ref, s16l_ref, 16)
    m = jnp.maximum(jnp.maximum(l1, l4), l16)
    e1, e4, e16 = jnp.exp(l1 - m), jnp.exp(l4 - m), jnp.exp(l16 - m)
    yc = ((e1 * o1 + e4 * o4 + e16 * o16) / (e1 + e4 + e16)).astype(BF16)

    h = x_ref[0] + _dot(jnp.concatenate([ya, yb_ref[0], yc], axis=1), wout_ref[...])
    hn = (h * lax.rsqrt(jnp.mean(h * h, axis=-1, keepdims=True) + NORM_EPS) * n2_ref[...]).astype(BF16)

    y = h
    c0 = 0
    for width in FF_GROUPS:
        cols = slice(c0, c0 + width)
        act = (_silu(_dot(hn, wg_ref[:, cols])) * _dot(hn, wu_ref[:, cols])).astype(BF16)
        y = y + _dot(act, wd_ref[cols, :])
        c0 += width
    y_ref[0] = y


def _outffn_call(x, o_f, o_b, a_raw, ag, yb, c_out, w_out, n2, wg, wu, wd):
    n, t, _ = x.shape
    tm = TOKEN_TILE
    tok = lambda width, cb=0: pl.BlockSpec((1, tm, width), lambda b, i: (b, i, cb))
    dil = lambda d: pl.BlockSpec((1, d, tm // d, LANES), lambda b, i: (b, 0, i, 0))
    const = lambda shape: pl.BlockSpec(shape, lambda b, i: (0,) * len(shape), pipeline_mode=pl.Buffered(1))
    (o1, l1), (o4, l4), (o16, l16) = c_out
    return pl.pallas_call(
        _outffn_kernel,
        grid=(n, t // tm),
        in_specs=[tok(D_MODEL), tok(A_WIDTH), tok(A_WIDTH), tok(A_WIDTH, 4), const((1, LANES)), tok(3 * LANES),
                  tok(LANES), tok(LANES), dil(4), dil(4), dil(16), dil(16),
                  const(w_out.shape), const((1, D_MODEL)), const(wg.shape), const(wu.shape), const(wd.shape)],
        out_specs=tok(D_MODEL),
        out_shape=jax.ShapeDtypeStruct((n, t, D_MODEL), F32),
        scratch_shapes=[pltpu.VMEM((tm, LANES), F32)] * 4,
        compiler_params=pltpu.CompilerParams(dimension_semantics=("arbitrary", "arbitrary"),
                                             vmem_limit_bytes=VMEM_LIMIT),
        name="outffn",
    )(x, o_f, o_b, a_raw, ag, yb, o1, l1, o4, l4, o16, l16, w_out, n2, wg, wu, wd)


def _layer(x, p, layer, cos, sin):
    n, t, _ = x.shape
    outs = _proj_call(x, p["n1"], p["w_in"], cos, sin, p["bqg"], p["bkg"], p["cqg"], p["ckg"])
    a_raw, bq, bk, bv = outs[:4]
    o_f, o_b = _hgrn_call(a_raw, p["lb_raw"], layer)
    (yb,) = _attn_call(bq, bk, bv, p["sink"], w=B_HALF_WINDOW, heads=_B_HEADS, want_lse=False, name="attn_window")
    c_out = []
    for i, d in enumerate(C_DILATIONS):
        q, k, v = (a.reshape(n * d, t // d, LANES) for a in outs[4 + 3 * i:7 + 3 * i])
        o, lse = _attn_call(q, k, v, None, w=C_HALF_WINDOW, heads=_C_HEADS, want_lse=True, name=f"attn_dil{d}")
        shape = (n, t, LANES) if d == 1 else (n, d, t // d, LANES)
        c_out.append((o.reshape(shape), lse.reshape(shape)))
    return _outffn_call(x, o_f, o_b, a_raw, p["ag"], yb, c_out, p["w_out"], p["n2"], p["wg"], p["wu"], p["wd"])


def _layer_params(l, norm1_g, w_in, lb_raw, a_norm_g, b_qn_g, b_kn_g, b_sink, c_qn_g, c_kn_g, w_out, norm2_g,
                  w_gate, w_up, w_down):
    pair = lambda g: jnp.tile(g[l].astype(F32), LANES // HEAD_DIM)[None, :]
    return {
        "n1": norm1_g[l].astype(F32)[None, :],
        "w_in": w_in[l].astype(BF16),
        "lb_raw": lb_raw.astype(F32).reshape(lb_raw.shape[0], 2 * A_WIDTH),
        "ag": pair(a_norm_g), "bqg": pair(b_qn_g), "bkg": pair(b_kn_g), "cqg": pair(c_qn_g), "ckg": pair(c_kn_g),
        "sink": b_sink[l].astype(F32),
        "w_out": w_out[l].astype(BF16),
        "n2": norm2_g[l].astype(F32)[None, :],
        "wg": w_gate[l].astype(BF16),
        "wu": w_up[l].astype(BF16),
        "wd": w_down[l].astype(BF16),
    }


def kernel(x_prompt, x_sample, norm1_g, w_in, lb_raw, a_norm_g, b_qn_g, b_kn_g, b_sink, c_qn_g, c_kn_g, w_out,
           norm2_g, w_gate, w_up, w_down):
    depth = w_in.shape[0]
    params = [_layer_params(l, norm1_g, w_in, lb_raw, a_norm_g, b_qn_g, b_kn_g, b_sink, c_qn_g, c_kn_g, w_out,
                            norm2_g, w_gate, w_up, w_down) for l in range(depth)]

    def trunk(x):
        cos, sin = _rope_tables(x.shape[1])
        for l in range(depth):
            x = _layer(x, params[l], l, cos, sin)
        return x

    return (trunk(x_prompt), trunk(x_sample))
```

```python
import functools

import jax
import jax.numpy as jnp
import numpy as np
from jax import lax
from jax.experimental import pallas as pl
from jax.experimental.pallas import tpu as pltpu

F32 = jnp.float32
BF16 = jnp.bfloat16

D_MODEL = 1024
HEAD_DIM = 64
LANES = 128
A_WIDTH = 256
A_COLS = 5 * A_WIDTH
BQ_COL, BK_COL, BV_COL = 1280, 1664, 1792
CQ_COL, CK_COL, CV_COL = 1920, 2304, 2688
B_HALF_WINDOW = 128
C_HALF_WINDOW = 64
C_DILATIONS = (1, 4, 16)
D_FF = 2816
FF_GROUPS = (1024, 1024, 768)
ROPE_THETA = 10000.0
NORM_EPS = 1e-6
MASK_VALUE = -1e30
GATE_FLOOR = 1e-30
GLA_CHUNK = 64
GLA_SAFE_RANGE = 80.0

TOKEN_TILE = 512
HGRN_TILE = 512
ATTN_ROWS_WINDOW = 1024
ATTN_ROWS_DILATED = 2048
LOG2_E = 1.4426950408889634
VMEM_LIMIT = 56 * 1024 * 1024


def _dot(a, b):
    return jnp.dot(a, b, preferred_element_type=F32)


def _dot_nt(a, b):
    return lax.dot_general(a, b, (((1,), (1,)), ((), ())), preferred_element_type=F32)


def _dot_tn(a, b):
    return lax.dot_general(a, b, (((0,), (0,)), ((), ())), preferred_element_type=F32)


def _head_blockdiag(n):
    r = lax.broadcasted_iota(jnp.int32, (n, n), 0) >> 6
    c = lax.broadcasted_iota(jnp.int32, (n, n), 1) >> 6
    return (r == c).astype(BF16)


def _head_blockdiag2(n):
    bd = _head_blockdiag(n)
    return jnp.concatenate([bd, bd], axis=0)


def _head_sum(x, bd2):
    hi = x.astype(BF16)
    lo = (x - hi.astype(F32)).astype(BF16)
    return _dot(jnp.concatenate([hi, lo], axis=1), bd2)


def _head_rmsnorm(y, g, bd2):
    return y * lax.rsqrt(_head_sum(y * y, bd2) * (1.0 / HEAD_DIM) + NORM_EPS) * g


def _silu(x):
    return x * jax.nn.sigmoid(x)


def _proj_kernel(x_ref, g1_ref, w_ref, cos_ref, sin_ref, bqg_ref, bkg_ref, cqg_ref, ckg_ref,
                 a_ref, bq_ref, bk_ref, bv_ref,
                 cq1_ref, ck1_ref, cv1_ref, cq4_ref, ck4_ref, cv4_ref, cq16_ref, ck16_ref, cv16_ref,
                 scr_ref):
    tm = x_ref.shape[1]
    x = x_ref[0]
    xn = (x * lax.rsqrt(jnp.mean(x * x, axis=-1, keepdims=True) + NORM_EPS) * g1_ref[...]).astype(BF16)
    a_ref[0] = _dot(xn, w_ref[:, 0:A_COLS])
    y_bc = _dot(xn, w_ref[:, A_COLS:])

    bd = _head_blockdiag2(LANES)
    cos = cos_ref[...]
    sin = sin_ref[...]
    lane = lax.broadcasted_iota(jnp.int32, (1, LANES), 1)
    first_half = (lane & (HEAD_DIM - 1)) < (HEAD_DIM // 2)

    def proj(col):
        return y_bc[:, col - A_COLS:col - A_COLS + LANES]

    def norm_rope(col, g_ref):
        y = _head_rmsnorm(proj(col), g_ref[...], bd)
        partner = jnp.where(first_half, pltpu.roll(y, LANES - HEAD_DIM // 2, 1), pltpu.roll(y, HEAD_DIM // 2, 1))
        return y * cos + partner * sin

    scale = LOG2_E * HEAD_DIM ** -0.5
    for j in range(3):
        bq_ref[0, :, j * LANES:(j + 1) * LANES] = (norm_rope(BQ_COL + j * LANES, bqg_ref) * scale).astype(BF16)
    bk = norm_rope(BK_COL, bkg_ref)
    bk_ref[0, :, 0:LANES] = bk.astype(BF16)
    bk_ref[0, :, LANES:2 * LANES] = pltpu.roll(bk, HEAD_DIM, 1).astype(BF16)
    bv = proj(BV_COL)
    bv_ref[0, :, 0:LANES] = bv.astype(BF16)
    bv_ref[0, :, LANES:2 * LANES] = pltpu.roll(bv, HEAD_DIM, 1).astype(BF16)

    def store_dilated(ref, y, d):
        if d == 1:
            ref[0] = y.astype(BF16)
            return
        scr_ref[...] = y
        for r in range(d):
            ref[0, r] = scr_ref[pl.ds(r, tm // d, stride=d), :].astype(BF16)

    c_refs = ((cq1_ref, ck1_ref, cv1_ref), (cq4_ref, ck4_ref, cv4_ref), (cq16_ref, ck16_ref, cv16_ref))
    for p, d in enumerate(C_DILATIONS):
        q_ref, k_ref, v_ref = c_refs[p]
        store_dilated(q_ref, norm_rope(CQ_COL + p * LANES, cqg_ref) * scale, d)
        store_dilated(k_ref, norm_rope(CK_COL + p * LANES, ckg_ref), d)
        store_dilated(v_ref, proj(CV_COL + p * LANES), d)


def _rope_tables(t_len):
    half = HEAD_DIM // 2
    inv = ROPE_THETA ** (-jnp.arange(half, dtype=F32) / half)
    ang = jnp.arange(t_len).astype(F32)[:, None] * inv[None, :]
    cos = jnp.tile(jnp.cos(ang), (1, 2 * LANES // HEAD_DIM))
    sin = jnp.sin(ang)
    sin = jnp.tile(jnp.concatenate([-sin, sin], axis=1), (1, LANES // HEAD_DIM))
    return cos, sin


def _proj_call(x, g1, w_in, cos, sin, bqg, bkg, cqg, ckg):
    n, t, _ = x.shape
    tm = TOKEN_TILE
    grid = (n, t // tm)
    tok = lambda width: pl.BlockSpec((1, tm, width), lambda b, i: (b, i, 0))
    const = lambda shape: pl.BlockSpec(shape, lambda b, i: (0,) * len(shape))
    dil = lambda d: pl.BlockSpec((1, d, tm // d, LANES), lambda b, i: (b, 0, i, 0))
    out_shape = [jax.ShapeDtypeStruct((n, t, A_COLS), F32),
                 jax.ShapeDtypeStruct((n, t, 3 * LANES), BF16),
                 jax.ShapeDtypeStruct((n, t, 2 * LANES), BF16),
                 jax.ShapeDtypeStruct((n, t, 2 * LANES), BF16)]
    out_specs = [tok(A_COLS), tok(3 * LANES), tok(2 * LANES), tok(2 * LANES)]
    for d in C_DILATIONS:
        for _ in range(3):
            if d == 1:
                out_shape.append(jax.ShapeDtypeStruct((n, t, LANES), BF16))
                out_specs.append(tok(LANES))
            else:
                out_shape.append(jax.ShapeDtypeStruct((n, d, t // d, LANES), BF16))
                out_specs.append(dil(d))
    rope_spec = pl.BlockSpec((tm, LANES), lambda b, i: (i, 0))
    return pl.pallas_call(
        _proj_kernel,
        grid=grid,
        in_specs=[tok(D_MODEL), const((1, D_MODEL)), const(w_in.shape), rope_spec, rope_spec,
                  const((1, LANES)), const((1, LANES)), const((1, LANES)), const((1, LANES))],
        out_specs=out_specs,
        out_shape=out_shape,
        scratch_shapes=[pltpu.VMEM((tm, LANES), F32)],
        compiler_params=pltpu.CompilerParams(dimension_semantics=("arbitrary", "arbitrary"),
                                             vmem_limit_bytes=VMEM_LIMIT),
        name="proj",
    )(x, g1, w_in, cos, sin, bqg, bkg, cqg, ckg)


def _hgrn_chunk(q, f_raw, v, lb, state_ref, reverse, exact_refs=None):
    c = GLA_CHUNK
    s = jax.nn.sigmoid(f_raw)
    f = lb + (1.0 - lb) * s
    g = jnp.log(jnp.maximum(f, GATE_FLOOR))
    kk = (1.0 - lb) * (1.0 - s)

    row = lax.broadcasted_iota(jnp.int32, (c, 3 * c), 0)
    col = lax.broadcasted_iota(jnp.int32, (c, 3 * c), 1) & (c - 1)
    tri3 = ((col >= row) if reverse else (col <= row)).astype(BF16)
    g1 = g.astype(BF16)
    r1 = g - g1.astype(F32)
    g2 = r1.astype(BF16)
    g3 = (r1 - g2.astype(F32)).astype(BF16)
    b = _dot(tri3, jnp.concatenate([g1, g2, g3], axis=0))
    total = b[0:1] if reverse else b[c - 1:c]
    mid = b[c // 2:c // 2 + 1] if reverse else b[c // 2 - 1:c // 2]

    lane_head = lax.broadcasted_iota(jnp.int32, (1, A_WIDTH), 1) >> 6
    v16 = v.astype(BF16)

    state = state_ref[...]
    o = _dot_nt((q * jnp.exp(b)).astype(BF16), state.astype(BF16))
    upd = _dot_tn(v16, (kk * jnp.exp(total - b)).astype(BF16))
    srow = lax.broadcasted_iota(jnp.int32, (A_WIDTH, A_WIDTH), 0) >> 6
    scol = lax.broadcasted_iota(jnp.int32, (A_WIDTH, A_WIDTH), 1) >> 6
    state_ref[...] = state * jnp.exp(total) + jnp.where(srow == scol, upd, 0.0)

    offset = jnp.max(jnp.abs(b - mid), axis=0, keepdims=True)

    if exact_refs is None:
        qp = q * jnp.exp(b - mid)
        kp = (kk * jnp.exp(mid - b)).astype(BF16)
        q_stack = jnp.concatenate([jnp.where(lane_head == h, qp, 0.0) for h in range(4)], axis=0).astype(BF16)
        a = _dot_nt(q_stack, kp)
        t_idx = lax.broadcasted_iota(jnp.int32, (4 * c, c), 0) & (c - 1)
        s_idx = lax.broadcasted_iota(jnp.int32, (4 * c, c), 1)
        a = jnp.where((s_idx >= t_idx) if reverse else (s_idx <= t_idx), a, 0.0)
        a16 = a.astype(BF16)
        for h in range(4):
            o = o + _dot(a16[h * c:(h + 1) * c], jnp.where(lane_head == h, v16, 0))
        return o, offset

    tb_ref, tk_ref, tv_ref = exact_refs
    tb_ref[...] = b
    tk_ref[...] = kk
    tv_ref[...] = v
    bd = _head_blockdiag(A_WIDTH)
    t_col = lax.broadcasted_iota(jnp.int32, (c, 1), 0)

    def body(si, acc):
        b_s = tb_ref[pl.ds(si, 1), :]
        dec = jnp.exp(jnp.minimum(b - b_s, 0.0))
        valid = (t_col <= si) if reverse else (t_col >= si)
        p = jnp.where(valid, q * dec * tk_ref[pl.ds(si, 1), :], 0.0)
        return acc + _dot(p.astype(BF16), bd) * tv_ref[pl.ds(si, 1), :]

    return lax.fori_loop(0, c, body, o), offset


def _hgrn_kernel(lb_ref, qf_ref, ff_ref, vf_ref, qb_ref, fb_ref, vb_ref, of_ref, ob_ref,
                 sf_ref, sb_ref, sf0_ref, sb0_ref, tb_ref, tk_ref, tv_ref, *, layer, n_chunks):
    @pl.when(pl.program_id(1) == 0)
    def _():
        sf_ref[...] = jnp.zeros_like(sf_ref)
        sb_ref[...] = jnp.zeros_like(sb_ref)

    sf0_ref[...] = sf_ref[...]
    sb0_ref[...] = sb_ref[...]

    raw = lb_ref[...]
    depth = raw.shape[0]
    mx = raw[0:1]
    for i in range(1, depth):
        mx = jnp.maximum(mx, raw[i:i + 1])
    e = jnp.exp(raw - mx)
    den = e[0:1]
    for i in range(1, depth):
        den = den + e[i:i + 1]
    sm = e / den
    cs = sm[0:1]
    for i in range(1, layer + 1):
        cs = cs + sm[i:i + 1]
    lb = cs - sm[0:1]
    lb_f = lb[:, 0:A_WIDTH]
    lb_b = lb[:, A_WIDTH:2 * A_WIDTH]

    def chunk_rows(ci):
        start = ci * GLA_CHUNK
        return pl.ds(start if isinstance(ci, int) else pl.multiple_of(start, GLA_CHUNK), GLA_CHUNK)

    def run_chunk(ci, exact_refs):
        rows = chunk_rows(ci)
        o_f, off_f = _hgrn_chunk(qf_ref[0, rows, :], ff_ref[0, rows, :], vf_ref[0, rows, :], lb_f, sf_ref,
                                 reverse=False, exact_refs=exact_refs)
        of_ref[0, rows, :] = o_f
        rows = chunk_rows(n_chunks - 1 - ci)
        o_b, off_b = _hgrn_chunk(qb_ref[0, rows, :], fb_ref[0, rows, :], vb_ref[0, rows, :], lb_b, sb_ref,
                                 reverse=True, exact_refs=exact_refs)
        ob_ref[0, rows, :] = o_b
        return jnp.maximum(off_f, off_b)

    worst = jnp.zeros((1, A_WIDTH), F32)
    for ci in range(n_chunks):
        worst = jnp.maximum(worst, run_chunk(ci, None))

    @pl.when(jnp.max(worst) > GLA_SAFE_RANGE)
    def _():
        sf_ref[...] = sf0_ref[...]
        sb_ref[...] = sb0_ref[...]

        def step(ci, carry):
            run_chunk(ci, (tb_ref, tk_ref, tv_ref))
            return carry

        lax.fori_loop(0, n_chunks, step, 0)


def _hgrn_call(a_raw, lb_raw2, layer):
    n, t, _ = a_raw.shape
    tt = HGRN_TILE
    nt = t // tt
    fwd = lambda cb: pl.BlockSpec((1, tt, A_WIDTH), lambda b, i: (b, i, cb))
    bwd = lambda cb: pl.BlockSpec((1, tt, A_WIDTH), lambda b, i: (b, nt - 1 - i, cb))
    return pl.pallas_call(
        functools.partial(_hgrn_kernel, layer=layer, n_chunks=tt // GLA_CHUNK),
        grid=(n, nt),
        in_specs=[pl.BlockSpec(lb_raw2.shape, lambda b, i: (0, 0)),
                  fwd(0), fwd(1), fwd(3), bwd(0), bwd(2), bwd(3)],
        out_specs=[fwd(0), bwd(0)],
        out_shape=[jax.ShapeDtypeStruct((n, t, A_WIDTH), F32)] * 2,
        scratch_shapes=[pltpu.VMEM((A_WIDTH, A_WIDTH), F32)] * 4 + [pltpu.VMEM((GLA_CHUNK, A_WIDTH), F32)] * 3,
        compiler_params=pltpu.CompilerParams(dimension_semantics=("arbitrary", "arbitrary"),
                                             vmem_limit_bytes=VMEM_LIMIT),
        name="hgrn",
    )(lb_raw2, a_raw, a_raw, a_raw, a_raw, a_raw, a_raw)


def _attn_kernel(*refs, w, seq_len, heads, use_sink, want_lse):
    q_ref, kp_ref, kc_ref, kn_ref, vp_ref, vc_ref, vn_ref = refs[:7]
    rest = list(refs[7:])
    sink_ref = rest.pop(0) if use_sink else None
    o_ref = rest.pop(0)
    lse_ref = rest.pop(0) if want_lse else None

    n_seq, tr = q_ref.shape[0], q_ref.shape[1]
    sb_rows, sb_keys = 2 * w, 4 * w
    n_sb = tr // sb_rows
    t0 = pl.program_id(1) * tr
    lane_half = lax.broadcasted_iota(jnp.int32, (1, LANES), 1) >> 6
    row = lax.broadcasted_iota(jnp.int32, (sb_rows, sb_keys), 0)
    col = lax.broadcasted_iota(jnp.int32, (sb_rows, sb_keys), 1)
    band = (col >= row) & (col - row <= 2 * w)

    for sq in range(n_seq):
        k_ext = jnp.concatenate([kp_ref[sq], kc_ref[sq], kn_ref[sq]], axis=0)
        v_ext = jnp.concatenate([vp_ref[sq], vc_ref[sq], vn_ref[sq]], axis=0)
        kv = {}
        for blk_heads in heads:
            for half, kv_blk in blk_heads:
                if (half, kv_blk) not in kv:
                    sel = lane_half == half
                    kv[(half, kv_blk)] = (jnp.where(sel, k_ext[:, kv_blk * LANES:(kv_blk + 1) * LANES], 0),
                                          jnp.where(sel, v_ext[:, kv_blk * LANES:(kv_blk + 1) * LANES], 0))

        for sb in range(n_sb):
            r0 = sb * sb_rows
            mask = band
            if sb == 0 or sb == n_sb - 1:
                key_pos = col + (t0 - w + r0)
                mask = band & (key_pos >= 0) & (key_pos < seq_len)
            for j, blk_heads in enumerate(heads):
                q = q_ref[sq, r0:r0 + sb_rows, j * LANES:(j + 1) * LANES]
                out = None
                lse_out = None
                for e, (half, kv_blk) in enumerate(blk_heads):
                    k_h, v_h = kv[(half, kv_blk)]
                    s = jnp.where(mask, _dot_nt(q, k_h[r0:r0 + sb_keys]), MASK_VALUE)
                    m = jnp.max(s, axis=-1, keepdims=True)
                    if use_sink:
                        sink = sink_ref[2 * j + e] * LOG2_E
                        m = jnp.maximum(m, sink)
                    p = jnp.exp2(s - m)
                    den = jnp.sum(p, axis=-1, keepdims=True)
                    if use_sink:
                        den = den + jnp.exp2(sink - m)
                    o = _dot(p.astype(BF16), v_h[r0:r0 + sb_keys]) * (1.0 / den)
                    out = o if out is None else out + o
                    if want_lse:
                        lse = m + jnp.log2(den)
                        lse_out = lse if lse_out is None else jnp.where(lane_half == half, lse, lse_out)
                o_ref[sq, r0:r0 + sb_rows, j * LANES:(j + 1) * LANES] = out.astype(o_ref.dtype)
                if want_lse:
                    lse_ref[sq, r0:r0 + sb_rows, :] = jnp.broadcast_to(lse_out, (sb_rows, LANES))


def _attn_call(q, k, v, sink, *, w, heads, want_lse, rows_per_step, name):
    n, seq_len, qw = q.shape
    kw = k.shape[-1]
    tr = min(rows_per_step, seq_len)
    nb = min(rows_per_step // tr, n)
    assert seq_len % tr == 0 and n % nb == 0 and tr % (2 * w) == 0
    per = tr // w
    last = seq_len // w - 1
    cur = lambda width: pl.BlockSpec((nb, tr, width), lambda b, i: (b, i, 0))
    prev = pl.BlockSpec((nb, w, kw), lambda b, i: (b, jnp.maximum(i * per - 1, 0), 0))
    nxt = pl.BlockSpec((nb, w, kw), lambda b, i: (b, jnp.minimum((i + 1) * per, last), 0))
    in_specs = [cur(qw), prev, cur(kw), nxt, prev, cur(kw), nxt]
    args = [q, k, k, k, v, v, v]
    if sink is not None:
        in_specs.append(pl.BlockSpec(memory_space=pltpu.SMEM))
        args.append(sink)
    out_shape = [jax.ShapeDtypeStruct((n, seq_len, qw), BF16)]
    out_specs = [cur(qw)]
    if want_lse:
        out_shape.append(jax.ShapeDtypeStruct((n, seq_len, LANES), F32))
        out_specs.append(cur(LANES))
    return pl.pallas_call(
        functools.partial(_attn_kernel, w=w, seq_len=seq_len, heads=heads, use_sink=sink is not None,
                          want_lse=want_lse),
        grid=(n // nb, seq_len // tr),
        in_specs=in_specs,
        out_specs=out_specs,
        out_shape=out_shape,
        compiler_params=pltpu.CompilerParams(dimension_semantics=("arbitrary", "arbitrary"),
                                             vmem_limit_bytes=VMEM_LIMIT),
        name=name,
    )(*args)


_B_HEADS = tuple(tuple((h % 2, 0 if (h // 3) == (h % 2) else 1) for h in (2 * j, 2 * j + 1)) for j in range(3))
_C_HEADS = (((0, 0), (1, 0)),)


def _outffn_kernel(x_ref, of_ref, ob_ref, gate_ref, ag_ref, yb_ref,
                   o1_ref, l1_ref, o4_ref, l4_ref, o16_ref, l16_ref,
                   wout_ref, n2_ref, wg_ref, wu_ref, wd_ref,
                   y_ref, s4o_ref, s4l_ref, s16o_ref, s16l_ref):
    tm = x_ref.shape[1]
    bd = _head_blockdiag2(LANES)
    ya = []
    for j in range(A_WIDTH // LANES):
        cols = slice(j * LANES, (j + 1) * LANES)
        o = of_ref[0, :, cols] + ob_ref[0, :, cols]
        ya.append(_head_rmsnorm(o, ag_ref[...], bd) * _silu(gate_ref[0, :, cols]))
    ya = jnp.concatenate(ya, axis=1).astype(BF16)

    def interleave(o_ref, l_ref, so_ref, sl_ref, d):
        for r in range(d):
            so_ref[pl.ds(r, tm // d, stride=d), :] = o_ref[0, r].astype(F32)
            sl_ref[pl.ds(r, tm // d, stride=d), :] = l_ref[0, r]
        return so_ref[...], sl_ref[...]

    o1, l1 = o1_ref[0].astype(F32), l1_ref[0]
    o4, l4 = interleave(o4_ref, l4_ref, s4o_ref, s4l_ref, 4)
    o16, l16 = interleave(o16_ref, l16_ref, s16o_ref, s16l_ref, 16)
    m = jnp.maximum(jnp.maximum(l1, l4), l16)
    e1, e4, e16 = jnp.exp2(l1 - m), jnp.exp2(l4 - m), jnp.exp2(l16 - m)
    yc = ((e1 * o1 + e4 * o4 + e16 * o16) / (e1 + e4 + e16)).astype(BF16)

    h = x_ref[0] + _dot(jnp.concatenate([ya, yb_ref[0], yc], axis=1), wout_ref[...])
    hn = (h * lax.rsqrt(jnp.mean(h * h, axis=-1, keepdims=True) + NORM_EPS) * n2_ref[...]).astype(BF16)

    y = h
    c0 = 0
    for width in FF_GROUPS:
        cols = slice(c0, c0 + width)
        act = (_silu(_dot(hn, wg_ref[:, cols])) * _dot(hn, wu_ref[:, cols])).astype(BF16)
        y = y + _dot(act, wd_ref[cols, :])
        c0 += width
    y_ref[0] = y


def _outffn_call(x, o_f, o_b, a_raw, ag, yb, c_out, w_out, n2, wg, wu, wd):
    n, t, _ = x.shape
    tm = TOKEN_TILE
    tok = lambda width, cb=0: pl.BlockSpec((1, tm, width), lambda b, i: (b, i, cb))
    dil = lambda d: pl.BlockSpec((1, d, tm // d, LANES), lambda b, i: (b, 0, i, 0))
    const = lambda shape: pl.BlockSpec(shape, lambda b, i: (0,) * len(shape), pipeline_mode=pl.Buffered(1))
    (o1, l1), (o4, l4), (o16, l16) = c_out
    return pl.pallas_call(
        _outffn_kernel,
        grid=(n, t // tm),
        in_specs=[tok(D_MODEL), tok(A_WIDTH), tok(A_WIDTH), tok(A_WIDTH, 4), const((1, LANES)), tok(3 * LANES),
                  tok(LANES), tok(LANES), dil(4), dil(4), dil(16), dil(16),
                  const(w_out.shape), const((1, D_MODEL)), const(wg.shape), const(wu.shape), const(wd.shape)],
        out_specs=tok(D_MODEL),
        out_shape=jax.ShapeDtypeStruct((n, t, D_MODEL), F32),
        scratch_shapes=[pltpu.VMEM((tm, LANES), F32)] * 4,
        compiler_params=pltpu.CompilerParams(dimension_semantics=("arbitrary", "arbitrary"),
                                             vmem_limit_bytes=VMEM_LIMIT),
        name="outffn",
    )(x, o_f, o_b, a_raw, ag, yb, o1, l1, o4, l4, o16, l16, w_out, n2, wg, wu, wd)


def _layer(x, p, layer, cos, sin):
    n, t, _ = x.shape
    outs = _proj_call(x, p["n1"], p["w_in"], cos, sin, p["bqg"], p["bkg"], p["cqg"], p["ckg"])
    a_raw, bq, bk, bv = outs[:4]
    o_f, o_b = _hgrn_call(a_raw, p["lb_raw"], layer)
    (yb,) = _attn_call(bq, bk, bv, p["sink"], w=B_HALF_WINDOW, heads=_B_HEADS, want_lse=False,
                       rows_per_step=ATTN_ROWS_WINDOW, name="attn_window")
    c_out = []
    for i, d in enumerate(C_DILATIONS):
        q, k, v = (a.reshape(n * d, t // d, LANES) for a in outs[4 + 3 * i:7 + 3 * i])
        o, lse = _attn_call(q, k, v, None, w=C_HALF_WINDOW, heads=_C_HEADS, want_lse=True,
                            rows_per_step=ATTN_ROWS_DILATED, name=f"attn_dil{d}")
        shape = (n, t, LANES) if d == 1 else (n, d, t // d, LANES)
        c_out.append((o.reshape(shape), lse.reshape(shape)))
    return _outffn_call(x, o_f, o_b, a_raw, p["ag"], yb, c_out, p["w_out"], p["n2"], p["wg"], p["wu"], p["wd"])


def _layer_params(l, norm1_g, w_in, lb_raw, a_norm_g, b_qn_g, b_kn_g, b_sink, c_qn_g, c_kn_g, w_out, norm2_g,
                  w_gate, w_up, w_down):
    pair = lambda g: jnp.tile(g[l].astype(F32), LANES // HEAD_DIM)[None, :]
    return {
        "n1": norm1_g[l].astype(F32)[None, :],
        "w_in": w_in[l].astype(BF16),
        "lb_raw": lb_raw.astype(F32).reshape(lb_raw.shape[0], 2 * A_WIDTH),
        "ag": pair(a_norm_g), "bqg": pair(b_qn_g), "bkg": pair(b_kn_g), "cqg": pair(c_qn_g), "ckg": pair(c_kn_g),
        "sink": b_sink[l].astype(F32),
        "w_out": w_out[l].astype(BF16),
        "n2": norm2_g[l].astype(F32)[None, :],
        "wg": w_gate[l].astype(BF16),
        "wu": w_up[l].astype(BF16),
        "wd": w_down[l].astype(BF16),
    }


def kernel(x_prompt, x_sample, norm1_g, w_in, lb_raw, a_norm_g, b_qn_g, b_kn_g, b_sink, c_qn_g, c_kn_g, w_out,
           norm2_g, w_gate, w_up, w_down):
    depth = w_in.shape[0]
    params = [_layer_params(l, norm1_g, w_in, lb_raw, a_norm_g, b_qn_g, b_kn_g, b_sink, c_qn_g, c_kn_g, w_out,
                            norm2_g, w_gate, w_up, w_down) for l in range(depth)]

    def trunk(x):
        cos, sin = _rope_tables(x.shape[1])
        for l in range(depth):
            x = _layer(x, params[l], l, cos, sin)
        return x

    return (trunk(x_prompt), trunk(x_sample))
```

```python
import functools

import jax
import jax.numpy as jnp
import numpy as np
from jax import lax
from jax.experimental import pallas as pl
from jax.experimental.pallas import tpu as pltpu

F32 = jnp.float32
BF16 = jnp.bfloat16

D_MODEL = 1024
HEAD_DIM = 64
LANES = 128
A_WIDTH = 256
A_COLS = 5 * A_WIDTH
BQ_COL, BK_COL, BV_COL = 1280, 1664, 1792
CQ_COL, CK_COL, CV_COL = 1920, 2304, 2688
B_HALF_WINDOW = 128
C_HALF_WINDOW = 64
C_DILATIONS = (1, 4, 16)
D_FF = 2816
FF_GROUPS = (1024, 1024, 768)
ROPE_THETA = 10000.0
NORM_EPS = 1e-6
MASK_VALUE = -1e30
GATE_FLOOR = 1e-30
GLA_CHUNK = 64
GLA_SAFE_RANGE = 80.0

TOKEN_TILE = 512
HGRN_TILE = 512
ATTN_ROWS_WINDOW = 1024
ATTN_ROWS_DILATED = 2048
HGRN_STAGE_SKEW = 2
ATTN_SUB_ROWS = 128
ATTN_STAGE_SKEW = 1
LOG2_E = 1.4426950408889634
VMEM_LIMIT = 56 * 1024 * 1024


def _dot(a, b):
    return jnp.dot(a, b, preferred_element_type=F32)


def _dot_nt(a, b):
    return lax.dot_general(a, b, (((1,), (1,)), ((), ())), preferred_element_type=F32)


def _dot_tn(a, b):
    return lax.dot_general(a, b, (((0,), (0,)), ((), ())), preferred_element_type=F32)


def _head_blockdiag(n):
    r = lax.broadcasted_iota(jnp.int32, (n, n), 0) >> 6
    c = lax.broadcasted_iota(jnp.int32, (n, n), 1) >> 6
    return (r == c).astype(BF16)


def _head_blockdiag2(n):
    bd = _head_blockdiag(n)
    return jnp.concatenate([bd, bd], axis=0)


def _head_sum(x, bd2):
    hi = x.astype(BF16)
    lo = (x - hi.astype(F32)).astype(BF16)
    return _dot(jnp.concatenate([hi, lo], axis=1), bd2)


def _head_rmsnorm(y, g, bd2):
    return y * lax.rsqrt(_head_sum(y * y, bd2) * (1.0 / HEAD_DIM) + NORM_EPS) * g


def _silu(x):
    return x * jax.nn.sigmoid(x)


def _proj_kernel(x_ref, g1_ref, w_ref, cos_ref, sin_ref, bqg_ref, bkg_ref, cqg_ref, ckg_ref,
                 a_ref, bq_ref, bk_ref, bv_ref,
                 cq1_ref, ck1_ref, cv1_ref, cq4_ref, ck4_ref, cv4_ref, cq16_ref, ck16_ref, cv16_ref,
                 scr_ref):
    tm = x_ref.shape[1]
    x = x_ref[0]
    xn = (x * lax.rsqrt(jnp.mean(x * x, axis=-1, keepdims=True) + NORM_EPS) * g1_ref[...]).astype(BF16)
    y_bc = _dot(xn, w_ref[:, A_COLS:])

    bd = _head_blockdiag2(LANES)
    cos = cos_ref[...]
    sin = sin_ref[...]
    lane = lax.broadcasted_iota(jnp.int32, (1, LANES), 1)
    first_half = (lane & (HEAD_DIM - 1)) < (HEAD_DIM // 2)

    def proj(col):
        return y_bc[:, col - A_COLS:col - A_COLS + LANES]

    def norm_rope(col, g_ref):
        y = _head_rmsnorm(proj(col), g_ref[...], bd)
        partner = jnp.where(first_half, pltpu.roll(y, LANES - HEAD_DIM // 2, 1), pltpu.roll(y, HEAD_DIM // 2, 1))
        return y * cos + partner * sin

    scale = LOG2_E * HEAD_DIM ** -0.5
    for j in range(3):
        bq_ref[0, :, j * LANES:(j + 1) * LANES] = (norm_rope(BQ_COL + j * LANES, bqg_ref) * scale).astype(BF16)
    bk = norm_rope(BK_COL, bkg_ref)
    bk_ref[0, :, 0:LANES] = bk.astype(BF16)
    bk_ref[0, :, LANES:2 * LANES] = pltpu.roll(bk, HEAD_DIM, 1).astype(BF16)
    bv = proj(BV_COL)
    bv_ref[0, :, 0:LANES] = bv.astype(BF16)
    bv_ref[0, :, LANES:2 * LANES] = pltpu.roll(bv, HEAD_DIM, 1).astype(BF16)

    def store_dilated(ref, y, d, slot):
        if d == 1:
            ref[0] = y.astype(BF16)
            return
        scr_ref[slot] = y
        for r in range(d):
            ref[0, r] = scr_ref[slot, pl.ds(r, tm // d, stride=d), :].astype(BF16)

    c_refs = ((cq1_ref, ck1_ref, cv1_ref), (cq4_ref, ck4_ref, cv4_ref), (cq16_ref, ck16_ref, cv16_ref))
    for p, d in enumerate(C_DILATIONS):
        q_ref, k_ref, v_ref = c_refs[p]
        store_dilated(q_ref, norm_rope(CQ_COL + p * LANES, cqg_ref) * scale, d, 3 * p)
        store_dilated(k_ref, norm_rope(CK_COL + p * LANES, ckg_ref), d, 3 * p + 1)
        store_dilated(v_ref, proj(CV_COL + p * LANES), d, 3 * p + 2)

    a_ref[0] = _dot(xn, w_ref[:, 0:A_COLS])


def _rope_tables(t_len):
    half = HEAD_DIM // 2
    inv = ROPE_THETA ** (-jnp.arange(half, dtype=F32) / half)
    ang = jnp.arange(t_len).astype(F32)[:, None] * inv[None, :]
    cos = jnp.tile(jnp.cos(ang), (1, 2 * LANES // HEAD_DIM))
    sin = jnp.sin(ang)
    sin = jnp.tile(jnp.concatenate([-sin, sin], axis=1), (1, LANES // HEAD_DIM))
    return cos, sin


def _proj_call(x, g1, w_in, cos, sin, bqg, bkg, cqg, ckg):
    n, t, _ = x.shape
    tm = TOKEN_TILE
    grid = (n, t // tm)
    tok = lambda width: pl.BlockSpec((1, tm, width), lambda b, i: (b, i, 0))
    const = lambda shape: pl.BlockSpec(shape, lambda b, i: (0,) * len(shape))
    dil = lambda d: pl.BlockSpec((1, d, tm // d, LANES), lambda b, i: (b, 0, i, 0))
    out_shape = [jax.ShapeDtypeStruct((n, t, A_COLS), F32),
                 jax.ShapeDtypeStruct((n, t, 3 * LANES), BF16),
                 jax.ShapeDtypeStruct((n, t, 2 * LANES), BF16),
                 jax.ShapeDtypeStruct((n, t, 2 * LANES), BF16)]
    out_specs = [tok(A_COLS), tok(3 * LANES), tok(2 * LANES), tok(2 * LANES)]
    for d in C_DILATIONS:
        for _ in range(3):
            if d == 1:
                out_shape.append(jax.ShapeDtypeStruct((n, t, LANES), BF16))
                out_specs.append(tok(LANES))
            else:
                out_shape.append(jax.ShapeDtypeStruct((n, d, t // d, LANES), BF16))
                out_specs.append(dil(d))
    rope_spec = pl.BlockSpec((tm, LANES), lambda b, i: (i, 0))
    return pl.pallas_call(
        _proj_kernel,
        grid=grid,
        in_specs=[tok(D_MODEL), const((1, D_MODEL)), const(w_in.shape), rope_spec, rope_spec,
                  const((1, LANES)), const((1, LANES)), const((1, LANES)), const((1, LANES))],
        out_specs=out_specs,
        out_shape=out_shape,
        scratch_shapes=[pltpu.VMEM((3 * len(C_DILATIONS), tm, LANES), F32)],
        compiler_params=pltpu.CompilerParams(dimension_semantics=("arbitrary", "arbitrary"),
                                             vmem_limit_bytes=VMEM_LIMIT),
        name="proj",
    )(x, g1, w_in, cos, sin, bqg, bkg, cqg, ckg)


def _hgrn_gates(u):
    c = GLA_CHUNK
    reverse, lb = u["reverse"], u["lb"]
    q, f_raw, v = u["load"]()
    s = jax.nn.sigmoid(f_raw)
    f = lb + (1.0 - lb) * s
    g = jnp.log(jnp.maximum(f, GATE_FLOOR))
    kk = (1.0 - lb) * (1.0 - s)

    row = lax.broadcasted_iota(jnp.int32, (c, 3 * c), 0)
    col = lax.broadcasted_iota(jnp.int32, (c, 3 * c), 1) & (c - 1)
    tri3 = ((col >= row) if reverse else (col <= row)).astype(BF16)
    g1 = g.astype(BF16)
    r1 = g - g1.astype(F32)
    g2 = r1.astype(BF16)
    g3 = (r1 - g2.astype(F32)).astype(BF16)
    b = _dot(tri3, jnp.concatenate([g1, g2, g3], axis=0))
    u.update(q=q, v=v, kk=kk, b=b)


def _hgrn_state(u):
    c = GLA_CHUNK
    reverse, state_ref = u["reverse"], u["state_ref"]
    q, v, kk, b = u["q"], u["v"], u["kk"], u["b"]
    total = b[0:1] if reverse else b[c - 1:c]
    mid = b[c // 2:c // 2 + 1] if reverse else b[c // 2 - 1:c // 2]

    lane_head = lax.broadcasted_iota(jnp.int32, (1, A_WIDTH), 1) >> 6
    v16 = v.astype(BF16)

    state = state_ref[...]
    o = _dot_nt((q * jnp.exp(b)).astype(BF16), state.astype(BF16))
    upd = _dot_tn(v16, (kk * jnp.exp(total - b)).astype(BF16))
    srow = lax.broadcasted_iota(jnp.int32, (A_WIDTH, A_WIDTH), 0) >> 6
    scol = lax.broadcasted_iota(jnp.int32, (A_WIDTH, A_WIDTH), 1) >> 6
    state_ref[...] = state * jnp.exp(total) + jnp.where(srow == scol, upd, 0.0)

    u.update(o=o, v16=v16, offset=jnp.max(jnp.abs(b - mid), axis=0, keepdims=True))

    if u["exact_refs"] is None:
        qp = q * jnp.exp(b - mid)
        kp = (kk * jnp.exp(mid - b)).astype(BF16)
        q_stack = jnp.concatenate([jnp.where(lane_head == h, qp, 0.0) for h in range(4)], axis=0).astype(BF16)
        a = _dot_nt(q_stack, kp)
        t_idx = lax.broadcasted_iota(jnp.int32, (4 * c, c), 0) & (c - 1)
        s_idx = lax.broadcasted_iota(jnp.int32, (4 * c, c), 1)
        u["a16"] = jnp.where((s_idx >= t_idx) if reverse else (s_idx <= t_idx), a, 0.0).astype(BF16)


def _hgrn_output(u):
    c = GLA_CHUNK
    reverse = u["reverse"]
    o = u["o"]
    if u["exact_refs"] is None:
        lane_head = lax.broadcasted_iota(jnp.int32, (1, A_WIDTH), 1) >> 6
        for h in range(4):
            o = o + _dot(u["a16"][h * c:(h + 1) * c], jnp.where(lane_head == h, u["v16"], 0))
    else:
        q, b = u["q"], u["b"]
        tb_ref, tk_ref, tv_ref = u["exact_refs"]
        tb_ref[...] = b
        tk_ref[...] = u["kk"]
        tv_ref[...] = u["v"]
        bd = _head_blockdiag(A_WIDTH)
        t_col = lax.broadcasted_iota(jnp.int32, (c, 1), 0)

        def body(si, acc):
            b_s = tb_ref[pl.ds(si, 1), :]
            dec = jnp.exp(jnp.minimum(b - b_s, 0.0))
            valid = (t_col <= si) if reverse else (t_col >= si)
            p = jnp.where(valid, q * dec * tk_ref[pl.ds(si, 1), :], 0.0)
            return acc + _dot(p.astype(BF16), bd) * tv_ref[pl.ds(si, 1), :]

        o = lax.fori_loop(0, c, body, o)
    u["store"](o)


def _hgrn_kernel(lb_ref, qf_ref, ff_ref, vf_ref, qb_ref, fb_ref, vb_ref, of_ref, ob_ref,
                 sf_ref, sb_ref, sf0_ref, sb0_ref, tb_ref, tk_ref, tv_ref, *, layer, n_chunks):
    @pl.when(pl.program_id(1) == 0)
    def _():
        sf_ref[...] = jnp.zeros_like(sf_ref)
        sb_ref[...] = jnp.zeros_like(sb_ref)

    sf0_ref[...] = sf_ref[...]
    sb0_ref[...] = sb_ref[...]

    raw = lb_ref[...]
    depth = raw.shape[0]
    mx = raw[0:1]
    for i in range(1, depth):
        mx = jnp.maximum(mx, raw[i:i + 1])
    e = jnp.exp(raw - mx)
    den = e[0:1]
    for i in range(1, depth):
        den = den + e[i:i + 1]
    sm = e / den
    cs = sm[0:1]
    for i in range(1, layer + 1):
        cs = cs + sm[i:i + 1]
    lb = cs - sm[0:1]
    lb_f = lb[:, 0:A_WIDTH]
    lb_b = lb[:, A_WIDTH:2 * A_WIDTH]

    def chunk_rows(ci):
        start = ci * GLA_CHUNK
        return pl.ds(start if isinstance(ci, int) else pl.multiple_of(start, GLA_CHUNK), GLA_CHUNK)

    def chunk_pair(ci, exact_refs):
        def record(q_ref, f_ref, v_ref, o_ref, rows, lb_d, state_ref, reverse):
            def store(o):
                o_ref[0, rows, :] = o
            return {"load": lambda: (q_ref[0, rows, :], f_ref[0, rows, :], v_ref[0, rows, :]), "store": store,
                    "lb": lb_d, "state_ref": state_ref, "reverse": reverse, "exact_refs": exact_refs}
        return [record(qf_ref, ff_ref, vf_ref, of_ref, chunk_rows(ci), lb_f, sf_ref, False),
                record(qb_ref, fb_ref, vb_ref, ob_ref, chunk_rows(n_chunks - 1 - ci), lb_b, sb_ref, True)]

    units = [u for ci in range(n_chunks) for u in chunk_pair(ci, None)]
    stages = (_hgrn_gates, _hgrn_state, _hgrn_output)
    skew = HGRN_STAGE_SKEW
    for i in range(len(units) + skew * (len(stages) - 1)):
        for k, stage in enumerate(stages):
            if 0 <= i - k * skew < len(units):
                stage(units[i - k * skew])
    worst = units[0]["offset"]
    for u in units[1:]:
        worst = jnp.maximum(worst, u["offset"])

    @pl.when(jnp.max(worst) > GLA_SAFE_RANGE)
    def _():
        sf_ref[...] = sf0_ref[...]
        sb_ref[...] = sb0_ref[...]

        def step(ci, carry):
            for u in chunk_pair(ci, (tb_ref, tk_ref, tv_ref)):
                for stage in stages:
                    stage(u)
            return carry

        lax.fori_loop(0, n_chunks, step, 0)


def _hgrn_call(a_raw, lb_raw2, layer):
    n, t, _ = a_raw.shape
    tt = HGRN_TILE
    nt = t // tt
    fwd = lambda cb: pl.BlockSpec((1, tt, A_WIDTH), lambda b, i: (b, i, cb))
    bwd = lambda cb: pl.BlockSpec((1, tt, A_WIDTH), lambda b, i: (b, nt - 1 - i, cb))
    return pl.pallas_call(
        functools.partial(_hgrn_kernel, layer=layer, n_chunks=tt // GLA_CHUNK),
        grid=(n, nt),
        in_specs=[pl.BlockSpec(lb_raw2.shape, lambda b, i: (0, 0)),
                  fwd(0), fwd(1), fwd(3), bwd(0), bwd(2), bwd(3)],
        out_specs=[fwd(0), bwd(0)],
        out_shape=[jax.ShapeDtypeStruct((n, t, A_WIDTH), F32)] * 2,
        scratch_shapes=[pltpu.VMEM((A_WIDTH, A_WIDTH), F32)] * 4 + [pltpu.VMEM((GLA_CHUNK, A_WIDTH), F32)] * 3,
        compiler_params=pltpu.CompilerParams(dimension_semantics=("arbitrary", "arbitrary"),
                                             vmem_limit_bytes=VMEM_LIMIT),
        name="hgrn",
    )(lb_raw2, a_raw, a_raw, a_raw, a_raw, a_raw, a_raw)


def _attn_kernel(*refs, w, seq_len, heads, use_sink, want_lse):
    q_ref, kp_ref, kc_ref, kn_ref, vp_ref, vc_ref, vn_ref = refs[:7]
    rest = list(refs[7:])
    sink_ref = rest.pop(0) if use_sink else None
    o_ref = rest.pop(0)
    lse_ref = rest.pop(0) if want_lse else None

    n_seq, tr = q_ref.shape[0], q_ref.shape[1]
    sb_rows, sb_keys = ATTN_SUB_ROWS, ATTN_SUB_ROWS + 2 * w
    n_sb = tr // sb_rows
    t0 = pl.program_id(1) * tr
    lane_half = lax.broadcasted_iota(jnp.int32, (1, LANES), 1) >> 6
    row = lax.broadcasted_iota(jnp.int32, (sb_rows, sb_keys), 0)
    col = lax.broadcasted_iota(jnp.int32, (sb_rows, sb_keys), 1)
    band = (col >= row) & (col - row <= 2 * w)

    kv_cache = {}

    def kv_variants(sq):
        if sq not in kv_cache:
            k_ext = jnp.concatenate([kp_ref[sq], kc_ref[sq], kn_ref[sq]], axis=0)
            v_ext = jnp.concatenate([vp_ref[sq], vc_ref[sq], vn_ref[sq]], axis=0)
            kv = {}
            for blk_heads in heads:
                for half, kv_blk in blk_heads:
                    if (half, kv_blk) not in kv:
                        sel = lane_half == half
                        kv[(half, kv_blk)] = (jnp.where(sel, k_ext[:, kv_blk * LANES:(kv_blk + 1) * LANES], 0),
                                              jnp.where(sel, v_ext[:, kv_blk * LANES:(kv_blk + 1) * LANES], 0))
            kv_cache[sq] = kv
        return kv_cache[sq]

    mask_cache = {}

    def tile_mask(sb):
        if sb not in mask_cache:
            mask = band
            if sb == 0 or sb == n_sb - 1:
                key_pos = col + (t0 - w + sb * sb_rows)
                mask = band & (key_pos >= 0) & (key_pos < seq_len)
            mask_cache[sb] = mask
        return mask_cache[sb]

    def stage_scores(u):
        sq, sb, j = u["id"]
        r0 = sb * sb_rows
        q = q_ref[sq, r0:r0 + sb_rows, j * LANES:(j + 1) * LANES]
        u["s"], u["m"], u["sink"] = [], [], []
        for e, (half, kv_blk) in enumerate(heads[j]):
            k_h = kv_variants(sq)[(half, kv_blk)][0]
            s = jnp.where(tile_mask(sb), _dot_nt(q, k_h[r0:r0 + sb_keys]), MASK_VALUE)
            m = jnp.max(s, axis=-1, keepdims=True)
            sink = None
            if use_sink:
                sink = sink_ref[2 * j + e] * LOG2_E
                m = jnp.maximum(m, sink)
            u["s"].append(s)
            u["m"].append(m)
            u["sink"].append(sink)

    def stage_probs(u):
        u["p"], u["den"] = [], []
        for s, m, sink in zip(u["s"], u["m"], u["sink"]):
            p = jnp.exp2(s - m)
            den = jnp.sum(p, axis=-1, keepdims=True)
            if use_sink:
                den = den + jnp.exp2(sink - m)
            u["p"].append(p.astype(BF16))
            u["den"].append(den)
        u["s"] = None

    def stage_out(u):
        sq, sb, j = u["id"]
        r0 = sb * sb_rows
        out = None
        lse_out = None
        for e, (half, kv_blk) in enumerate(heads[j]):
            v_h = kv_variants(sq)[(half, kv_blk)][1]
            o = _dot(u["p"][e], v_h[r0:r0 + sb_keys]) * (1.0 / u["den"][e])
            out = o if out is None else out + o
            if want_lse:
                lse = u["m"][e] + jnp.log2(u["den"][e])
                lse_out = lse if lse_out is None else jnp.where(lane_half == half, lse, lse_out)
        o_ref[sq, r0:r0 + sb_rows, j * LANES:(j + 1) * LANES] = out.astype(o_ref.dtype)
        if want_lse:
            lse_ref[sq, r0:r0 + sb_rows, :] = jnp.broadcast_to(lse_out, (sb_rows, LANES))

    units = [{"id": (sq, sb, j)} for sq in range(n_seq) for sb in range(n_sb) for j in range(len(heads))]
    skew = ATTN_STAGE_SKEW
    for i in range(len(units) + 2 * skew):
        if i < len(units):
            stage_scores(units[i])
        if 0 <= i - skew < len(units):
            stage_probs(units[i - skew])
        if 0 <= i - 2 * skew < len(units):
            stage_out(units[i - 2 * skew])


def _attn_call(q, k, v, sink, *, w, heads, want_lse, rows_per_step, name):
    n, seq_len, qw = q.shape
    kw = k.shape[-1]
    tr = min(rows_per_step, seq_len)
    nb = min(rows_per_step // tr, n)
    assert seq_len % tr == 0 and n % nb == 0 and tr % ATTN_SUB_ROWS == 0 and tr % w == 0
    per = tr // w
    last = seq_len // w - 1
    cur = lambda width: pl.BlockSpec((nb, tr, width), lambda b, i: (b, i, 0))
    prev = pl.BlockSpec((nb, w, kw), lambda b, i: (b, jnp.maximum(i * per - 1, 0), 0))
    nxt = pl.BlockSpec((nb, w, kw), lambda b, i: (b, jnp.minimum((i + 1) * per, last), 0))
    in_specs = [cur(qw), prev, cur(kw), nxt, prev, cur(kw), nxt]
    args = [q, k, k, k, v, v, v]
    if sink is not None:
        in_specs.append(pl.BlockSpec(memory_space=pltpu.SMEM))
        args.append(sink)
    out_shape = [jax.ShapeDtypeStruct((n, seq_len, qw), BF16)]
    out_specs = [cur(qw)]
    if want_lse:
        out_shape.append(jax.ShapeDtypeStruct((n, seq_len, LANES), F32))
        out_specs.append(cur(LANES))
    return pl.pallas_call(
        functools.partial(_attn_kernel, w=w, seq_len=seq_len, heads=heads, use_sink=sink is not None,
                          want_lse=want_lse),
        grid=(n // nb, seq_len // tr),
        in_specs=in_specs,
        out_specs=out_specs,
        out_shape=out_shape,
        compiler_params=pltpu.CompilerParams(dimension_semantics=("arbitrary", "arbitrary"),
                                             vmem_limit_bytes=VMEM_LIMIT),
        name=name,
    )(*args)


_B_HEADS = tuple(tuple((h % 2, 0 if (h // 3) == (h % 2) else 1) for h in (2 * j, 2 * j + 1)) for j in range(3))
_C_HEADS = (((0, 0), (1, 0)),)


def _outffn_kernel(x_ref, of_ref, ob_ref, gate_ref, ag_ref, yb_ref,
                   o1_ref, l1_ref, o4_ref, l4_ref, o16_ref, l16_ref,
                   wout_ref, n2_ref, wg_ref, wu_ref, wd_ref,
                   y_ref, s4o_ref, s4l_ref, s16o_ref, s16l_ref):
    tm = x_ref.shape[1]
    bd = _head_blockdiag2(LANES)
    ya = []
    for j in range(A_WIDTH // LANES):
        cols = slice(j * LANES, (j + 1) * LANES)
        o = of_ref[0, :, cols] + ob_ref[0, :, cols]
        ya.append(_head_rmsnorm(o, ag_ref[...], bd) * _silu(gate_ref[0, :, cols]))
    ya = jnp.concatenate(ya, axis=1).astype(BF16)

    def interleave(o_ref, l_ref, so_ref, sl_ref, d):
        for r in range(d):
            so_ref[pl.ds(r, tm // d, stride=d), :] = o_ref[0, r].astype(F32)
            sl_ref[pl.ds(r, tm // d, stride=d), :] = l_ref[0, r]
        return so_ref[...], sl_ref[...]

    o1, l1 = o1_ref[0].astype(F32), l1_ref[0]
    o4, l4 = interleave(o4_ref, l4_ref, s4o_ref, s4l_ref, 4)
    o16, l16 = interleave(o16_ref, l16_ref, s16o_ref, s16l_ref, 16)
    m = jnp.maximum(jnp.maximum(l1, l4), l16)
    e1, e4, e16 = jnp.exp2(l1 - m), jnp.exp2(l4 - m), jnp.exp2(l16 - m)
    yc = ((e1 * o1 + e4 * o4 + e16 * o16) / (e1 + e4 + e16)).astype(BF16)

    h = x_ref[0] + _dot(jnp.concatenate([ya, yb_ref[0], yc], axis=1), wout_ref[...])
    hn = (h * lax.rsqrt(jnp.mean(h * h, axis=-1, keepdims=True) + NORM_EPS) * n2_ref[...]).astype(BF16)

    y = h
    c0 = 0
    for width in FF_GROUPS:
        cols = slice(c0, c0 + width)
        act = (_silu(_dot(hn, wg_ref[:, cols])) * _dot(hn, wu_ref[:, cols])).astype(BF16)
        y = y + _dot(act, wd_ref[cols, :])
        c0 += width
    y_ref[0] = y


def _outffn_call(x, o_f, o_b, a_raw, ag, yb, c_out, w_out, n2, wg, wu, wd):
    n, t, _ = x.shape
    tm = TOKEN_TILE
    tok = lambda width, cb=0: pl.BlockSpec((1, tm, width), lambda b, i: (b, i, cb))
    dil = lambda d: pl.BlockSpec((1, d, tm // d, LANES), lambda b, i: (b, 0, i, 0))
    const = lambda shape: pl.BlockSpec(shape, lambda b, i: (0,) * len(shape), pipeline_mode=pl.Buffered(1))
    (o1, l1), (o4, l4), (o16, l16) = c_out
    return pl.pallas_call(
        _outffn_kernel,
        grid=(n, t // tm),
        in_specs=[tok(D_MODEL), tok(A_WIDTH), tok(A_WIDTH), tok(A_WIDTH, 4), const((1, LANES)), tok(3 * LANES),
                  tok(LANES), tok(LANES), dil(4), dil(4), dil(16), dil(16),
                  const(w_out.shape), const((1, D_MODEL)), const(wg.shape), const(wu.shape), const(wd.shape)],
        out_specs=tok(D_MODEL),
        out_shape=jax.ShapeDtypeStruct((n, t, D_MODEL), F32),
        scratch_shapes=[pltpu.VMEM((tm, LANES), F32)] * 4,
        compiler_params=pltpu.CompilerParams(dimension_semantics=("arbitrary", "arbitrary"),
                                             vmem_limit_bytes=VMEM_LIMIT),
        name="outffn",
    )(x, o_f, o_b, a_raw, ag, yb, o1, l1, o4, l4, o16, l16, w_out, n2, wg, wu, wd)


def _layer(x, p, layer, cos, sin):
    n, t, _ = x.shape
    outs = _proj_call(x, p["n1"], p["w_in"], cos, sin, p["bqg"], p["bkg"], p["cqg"], p["ckg"])
    a_raw, bq, bk, bv = outs[:4]
    o_f, o_b = _hgrn_call(a_raw, p["lb_raw"], layer)
    (yb,) = _attn_call(bq, bk, bv, p["sink"], w=B_HALF_WINDOW, heads=_B_HEADS, want_lse=False,
                       rows_per_step=ATTN_ROWS_WINDOW, name="attn_window")
    c_out = []
    for i, d in enumerate(C_DILATIONS):
        q, k, v = (a.reshape(n * d, t // d, LANES) for a in outs[4 + 3 * i:7 + 3 * i])
        o, lse = _attn_call(q, k, v, None, w=C_HALF_WINDOW, heads=_C_HEADS, want_lse=True,
                            rows_per_step=ATTN_ROWS_DILATED, name=f"attn_dil{d}")
        shape = (n, t, LANES) if d == 1 else (n, d, t // d, LANES)
        c_out.append((o.reshape(shape), lse.reshape(shape)))
    return _outffn_call(x, o_f, o_b, a_raw, p["ag"], yb, c_out, p["w_out"], p["n2"], p["wg"], p["wu"], p["wd"])


def _layer_params(l, norm1_g, w_in, lb_raw, a_norm_g, b_qn_g, b_kn_g, b_sink, c_qn_g, c_kn_g, w_out, norm2_g,
                  w_gate, w_up, w_down):
    pair = lambda g: jnp.tile(g[l].astype(F32), LANES // HEAD_DIM)[None, :]
    return {
        "n1": norm1_g[l].astype(F32)[None, :],
        "w_in": w_in[l].astype(BF16),
        "lb_raw": lb_raw.astype(F32).reshape(lb_raw.shape[0], 2 * A_WIDTH),
        "ag": pair(a_norm_g), "bqg": pair(b_qn_g), "bkg": pair(b_kn_g), "cqg": pair(c_qn_g), "ckg": pair(c_kn_g),
        "sink": b_sink[l].astype(F32),
        "w_out": w_out[l].astype(BF16),
        "n2": norm2_g[l].astype(F32)[None, :],
        "wg": w_gate[l].astype(BF16),
        "wu": w_up[l].astype(BF16),
        "wd": w_down[l].astype(BF16),
    }


def kernel(x_prompt, x_sample, norm1_g, w_in, lb_raw, a_norm_g, b_qn_g, b_kn_g, b_sink, c_qn_g, c_kn_g, w_out,
           norm2_g, w_gate, w_up, w_down):
    depth = w_in.shape[0]
    params = [_layer_params(l, norm1_g, w_in, lb_raw, a_norm_g, b_qn_g, b_kn_g, b_sink, c_qn_g, c_kn_g, w_out,
                            norm2_g, w_gate, w_up, w_down) for l in range(depth)]

    def trunk(x):
        cos, sin = _rope_tables(x.shape[1])
        for l in range(depth):
            x = _layer(x, params[l], l, cos, sin)
        return x

    return (trunk(x_prompt), trunk(x_sample))
```

```python
import functools

import jax
import jax.numpy as jnp
import numpy as np
from jax import lax
from jax.experimental import pallas as pl
from jax.experimental.pallas import tpu as pltpu

F32 = jnp.float32
BF16 = jnp.bfloat16

D_MODEL = 1024
HEAD_DIM = 64
LANES = 128
A_WIDTH = 256
A_COLS = 5 * A_WIDTH
BQ_COL, BK_COL, BV_COL = 1280, 1664, 1792
CQ_COL, CK_COL, CV_COL = 1920, 2304, 2688
B_HALF_WINDOW = 128
C_HALF_WINDOW = 64
C_DILATIONS = (1, 4, 16)
D_FF = 2816
FF_GROUPS = (1024, 1024, 768)
ROPE_THETA = 10000.0
NORM_EPS = 1e-6
MASK_VALUE = -1e30
GATE_FLOOR = 1e-30
GLA_CHUNK = 64
GLA_SAFE_RANGE = 110.0

TOKEN_TILE = 512
HGRN_TILE = 512
ATTN_ROWS_WINDOW = 2048
ATTN_ROWS_DILATED = 2048
HGRN_STAGE_SKEW = 2
ATTN_SUB_ROWS = 128
ATTN_STAGE_SKEW = 1
LOG2_E = 1.4426950408889634
VMEM_LIMIT = 56 * 1024 * 1024


def _dot(a, b):
    return jnp.dot(a, b, preferred_element_type=F32)


def _dot_nt(a, b):
    return lax.dot_general(a, b, (((1,), (1,)), ((), ())), preferred_element_type=F32)


def _dot_tn(a, b):
    return lax.dot_general(a, b, (((0,), (0,)), ((), ())), preferred_element_type=F32)


def _head_blockdiag(n):
    r = lax.broadcasted_iota(jnp.int32, (n, n), 0) >> 6
    c = lax.broadcasted_iota(jnp.int32, (n, n), 1) >> 6
    return (r == c).astype(BF16)


def _head_blockdiag2(n):
    bd = _head_blockdiag(n)
    return jnp.concatenate([bd, bd], axis=0)


def _head_sum(x, bd2):
    hi = x.astype(BF16)
    lo = (x - hi.astype(F32)).astype(BF16)
    return _dot(jnp.concatenate([hi, lo], axis=1), bd2)


def _head_rmsnorm(y, g, bd2):
    return y * lax.rsqrt(_head_sum(y * y, bd2) * (1.0 / HEAD_DIM) + NORM_EPS) * g


def _silu(x):
    return x * jax.nn.sigmoid(x)


def _proj_kernel(x_ref, g1_ref, w_ref, cos_ref, sin_ref, bqg_ref, bkg_ref, cqg_ref, ckg_ref,
                 a_ref, bq_ref, bk_ref, bv_ref,
                 cq1_ref, ck1_ref, cv1_ref, cq4_ref, ck4_ref, cv4_ref, cq16_ref, ck16_ref, cv16_ref,
                 scr_ref):
    tm = x_ref.shape[1]
    x = x_ref[0]
    xn = (x * lax.rsqrt(jnp.mean(x * x, axis=-1, keepdims=True) + NORM_EPS) * g1_ref[...]).astype(BF16)
    y_bc = _dot(xn, w_ref[:, A_COLS:])

    bd = _head_blockdiag2(LANES)
    cos = cos_ref[...]
    sin = sin_ref[...]
    lane = lax.broadcasted_iota(jnp.int32, (1, LANES), 1)
    first_half = (lane & (HEAD_DIM - 1)) < (HEAD_DIM // 2)

    def proj(col):
        return y_bc[:, col - A_COLS:col - A_COLS + LANES]

    def norm_rope(col, g_ref):
        y = _head_rmsnorm(proj(col), g_ref[...], bd)
        partner = jnp.where(first_half, pltpu.roll(y, LANES - HEAD_DIM // 2, 1), pltpu.roll(y, HEAD_DIM // 2, 1))
        return y * cos + partner * sin

    scale = LOG2_E * HEAD_DIM ** -0.5
    for j in range(3):
        bq_ref[0, :, j * LANES:(j + 1) * LANES] = (norm_rope(BQ_COL + j * LANES, bqg_ref) * scale).astype(BF16)
    bk = norm_rope(BK_COL, bkg_ref)
    bk_ref[0, :, 0:LANES] = bk.astype(BF16)
    bk_ref[0, :, LANES:2 * LANES] = pltpu.roll(bk, HEAD_DIM, 1).astype(BF16)
    bv = proj(BV_COL)
    bv_ref[0, :, 0:LANES] = bv.astype(BF16)
    bv_ref[0, :, LANES:2 * LANES] = pltpu.roll(bv, HEAD_DIM, 1).astype(BF16)

    def store_dilated(ref, y, d, slot):
        if d == 1:
            ref[0] = y.astype(BF16)
            return
        scr_ref[slot] = y
        for r in range(d):
            ref[0, r] = scr_ref[slot, pl.ds(r, tm // d, stride=d), :].astype(BF16)

    c_refs = ((cq1_ref, ck1_ref, cv1_ref), (cq4_ref, ck4_ref, cv4_ref), (cq16_ref, ck16_ref, cv16_ref))
    for p, d in enumerate(C_DILATIONS):
        q_ref, k_ref, v_ref = c_refs[p]
        store_dilated(q_ref, norm_rope(CQ_COL + p * LANES, cqg_ref) * scale, d, 3 * p)
        store_dilated(k_ref, norm_rope(CK_COL + p * LANES, ckg_ref), d, 3 * p + 1)
        store_dilated(v_ref, proj(CV_COL + p * LANES), d, 3 * p + 2)

    a_ref[0] = _dot(xn, w_ref[:, 0:A_COLS])


def _rope_tables(t_len):
    half = HEAD_DIM // 2
    inv = ROPE_THETA ** (-jnp.arange(half, dtype=F32) / half)
    ang = jnp.arange(t_len).astype(F32)[:, None] * inv[None, :]
    cos = jnp.tile(jnp.cos(ang), (1, 2 * LANES // HEAD_DIM))
    sin = jnp.sin(ang)
    sin = jnp.tile(jnp.concatenate([-sin, sin], axis=1), (1, LANES // HEAD_DIM))
    return cos, sin


def _proj_call(x, g1, w_in, cos, sin, bqg, bkg, cqg, ckg):
    n, t, _ = x.shape
    tm = TOKEN_TILE
    grid = (n, t // tm)
    tok = lambda width: pl.BlockSpec((1, tm, width), lambda b, i: (b, i, 0))
    const = lambda shape: pl.BlockSpec(shape, lambda b, i: (0,) * len(shape))
    dil = lambda d: pl.BlockSpec((1, d, tm // d, LANES), lambda b, i: (b, 0, i, 0))
    out_shape = [jax.ShapeDtypeStruct((n, t, A_COLS), F32),
                 jax.ShapeDtypeStruct((n, t, 3 * LANES), BF16),
                 jax.ShapeDtypeStruct((n, t, 2 * LANES), BF16),
                 jax.ShapeDtypeStruct((n, t, 2 * LANES), BF16)]
    out_specs = [tok(A_COLS), tok(3 * LANES), tok(2 * LANES), tok(2 * LANES)]
    for d in C_DILATIONS:
        for _ in range(3):
            if d == 1:
                out_shape.append(jax.ShapeDtypeStruct((n, t, LANES), BF16))
                out_specs.append(tok(LANES))
            else:
                out_shape.append(jax.ShapeDtypeStruct((n, d, t // d, LANES), BF16))
                out_specs.append(dil(d))
    rope_spec = pl.BlockSpec((tm, LANES), lambda b, i: (i, 0))
    return pl.pallas_call(
        _proj_kernel,
        grid=grid,
        in_specs=[tok(D_MODEL), const((1, D_MODEL)), const(w_in.shape), rope_spec, rope_spec,
                  const((1, LANES)), const((1, LANES)), const((1, LANES)), const((1, LANES))],
        out_specs=out_specs,
        out_shape=out_shape,
        scratch_shapes=[pltpu.VMEM((3 * len(C_DILATIONS), tm, LANES), F32)],
        compiler_params=pltpu.CompilerParams(dimension_semantics=("arbitrary", "arbitrary"),
                                             vmem_limit_bytes=VMEM_LIMIT),
        name="proj",
    )(x, g1, w_in, cos, sin, bqg, bkg, cqg, ckg)


def _hgrn_gates(u):
    c = GLA_CHUNK
    reverse, lb = u["reverse"], u["lb"]
    q, f_raw, v = u["load"]()
    s = jax.nn.sigmoid(f_raw)
    f = lb + (1.0 - lb) * s
    g = jnp.log2(jnp.maximum(f, GATE_FLOOR))
    kk = (1.0 - lb) * (1.0 - s)

    row = lax.broadcasted_iota(jnp.int32, (c, 2 * c), 0)
    col = lax.broadcasted_iota(jnp.int32, (c, 2 * c), 1) & (c - 1)
    tri2 = ((col >= row) if reverse else (col <= row)).astype(BF16)
    g1 = g.astype(BF16)
    g2 = (g - g1.astype(F32)).astype(BF16)
    b = _dot(tri2, jnp.concatenate([g1, g2], axis=0))
    u.update(q=q, v=v, kk=kk, b=b)


def _hgrn_state(u):
    c = GLA_CHUNK
    reverse, state_ref = u["reverse"], u["state_ref"]
    q, v, kk, b = u["q"], u["v"], u["kk"], u["b"]
    total = b[0:1] if reverse else b[c - 1:c]
    mid = b[c // 2:c // 2 + 1] if reverse else b[c // 2 - 1:c // 2]
    v16 = v.astype(BF16)

    qs = (q * jnp.exp2(b)).astype(BF16)
    ks = (kk * jnp.exp2(total - b)).astype(BF16)
    decay = jnp.exp2(total)
    srow = lax.broadcasted_iota(jnp.int32, (LANES, LANES), 0) >> 6
    scol = lax.broadcasted_iota(jnp.int32, (LANES, LANES), 1) >> 6
    same_head = srow == scol
    o = []
    for j in range(A_WIDTH // LANES):
        cols = slice(j * LANES, (j + 1) * LANES)
        state = state_ref[j]
        o.append(_dot_nt(qs[:, cols], state.astype(BF16)))
        upd = _dot_tn(v16[:, cols], ks[:, cols])
        state_ref[j] = state * decay[:, cols] + jnp.where(same_head, upd, 0.0)

    offset = jnp.maximum(jnp.abs(b[0:1] - mid), jnp.abs(b[c - 1:c] - mid))
    u.update(o=o, v16=v16, offset=offset)

    if u["exact_refs"] is None:
        qp = q * jnp.exp2(b - mid)
        kp = (kk * jnp.exp2(mid - b)).astype(BF16)
        lane_half = lax.broadcasted_iota(jnp.int32, (1, LANES), 1) >> 6
        t_idx = lax.broadcasted_iota(jnp.int32, (2 * c, c), 0) & (c - 1)
        s_idx = lax.broadcasted_iota(jnp.int32, (2 * c, c), 1)
        causal = (s_idx >= t_idx) if reverse else (s_idx <= t_idx)
        u["a16"] = []
        for j in range(A_WIDTH // LANES):
            cols = slice(j * LANES, (j + 1) * LANES)
            q_stack = jnp.concatenate([jnp.where(lane_half == h, qp[:, cols], 0.0) for h in range(2)],
                                      axis=0).astype(BF16)
            a = _dot_nt(q_stack, kp[:, cols])
            u["a16"].append(jnp.where(causal, a, 0.0).astype(BF16))


def _hgrn_output(u):
    c = GLA_CHUNK
    reverse = u["reverse"]
    if u["exact_refs"] is None:
        lane_half = lax.broadcasted_iota(jnp.int32, (1, LANES), 1) >> 6
        o = []
        for j in range(A_WIDTH // LANES):
            v_blk = u["v16"][:, j * LANES:(j + 1) * LANES]
            acc = u["o"][j]
            for h in range(2):
                acc = acc + _dot(u["a16"][j][h * c:(h + 1) * c], jnp.where(lane_half == h, v_blk, 0))
            o.append(acc)
        o = jnp.concatenate(o, axis=1)
    else:
        q, b = u["q"], u["b"]
        tb_ref, tk_ref, tv_ref = u["exact_refs"]
        tb_ref[...] = b
        tk_ref[...] = u["kk"]
        tv_ref[...] = u["v"]
        bd = _head_blockdiag(A_WIDTH)
        t_col = lax.broadcasted_iota(jnp.int32, (c, 1), 0)

        def body(si, acc):
            b_s = tb_ref[pl.ds(si, 1), :]
            dec = jnp.exp2(jnp.minimum(b - b_s, 0.0))
            valid = (t_col <= si) if reverse else (t_col >= si)
            p = jnp.where(valid, q * dec * tk_ref[pl.ds(si, 1), :], 0.0)
            return acc + _dot(p.astype(BF16), bd) * tv_ref[pl.ds(si, 1), :]

        o = lax.fori_loop(0, c, body, jnp.concatenate(u["o"], axis=1))
    u["store"](o)


def _hgrn_kernel(lb_ref, qf_ref, ff_ref, vf_ref, qb_ref, fb_ref, vb_ref, of_ref, ob_ref,
                 sf_ref, sb_ref, sf0_ref, sb0_ref, tb_ref, tk_ref, tv_ref, *, layer, n_chunks):
    @pl.when(pl.program_id(1) == 0)
    def _():
        sf_ref[...] = jnp.zeros_like(sf_ref)
        sb_ref[...] = jnp.zeros_like(sb_ref)

    sf0_ref[...] = sf_ref[...]
    sb0_ref[...] = sb_ref[...]

    raw = lb_ref[...]
    depth = raw.shape[0]
    mx = raw[0:1]
    for i in range(1, depth):
        mx = jnp.maximum(mx, raw[i:i + 1])
    e = jnp.exp(raw - mx)
    den = e[0:1]
    for i in range(1, depth):
        den = den + e[i:i + 1]
    sm = e / den
    cs = sm[0:1]
    for i in range(1, layer + 1):
        cs = cs + sm[i:i + 1]
    lb = cs - sm[0:1]
    lb_f = lb[:, 0:A_WIDTH]
    lb_b = lb[:, A_WIDTH:2 * A_WIDTH]

    def chunk_rows(ci):
        start = ci * GLA_CHUNK
        return pl.ds(start if isinstance(ci, int) else pl.multiple_of(start, GLA_CHUNK), GLA_CHUNK)

    def chunk_pair(ci, exact_refs):
        def record(q_ref, f_ref, v_ref, o_ref, rows, lb_d, state_ref, reverse):
            def store(o):
                o_ref[0, rows, :] = o
            return {"load": lambda: (q_ref[0, rows, :], f_ref[0, rows, :], v_ref[0, rows, :]), "store": store,
                    "lb": lb_d, "state_ref": state_ref, "reverse": reverse, "exact_refs": exact_refs}
        return [record(qf_ref, ff_ref, vf_ref, of_ref, chunk_rows(ci), lb_f, sf_ref, False),
                record(qb_ref, fb_ref, vb_ref, ob_ref, chunk_rows(n_chunks - 1 - ci), lb_b, sb_ref, True)]

    units = [u for ci in range(n_chunks) for u in chunk_pair(ci, None)]
    stages = (_hgrn_gates, _hgrn_state, _hgrn_output)
    skew = HGRN_STAGE_SKEW
    for i in range(len(units) + skew * (len(stages) - 1)):
        for k, stage in enumerate(stages):
            if 0 <= i - k * skew < len(units):
                stage(units[i - k * skew])
    worst = units[0]["offset"]
    for u in units[1:]:
        worst = jnp.maximum(worst, u["offset"])

    @pl.when(jnp.max(worst) > GLA_SAFE_RANGE)
    def _():
        sf_ref[...] = sf0_ref[...]
        sb_ref[...] = sb0_ref[...]

        def step(ci, carry):
            for u in chunk_pair(ci, (tb_ref, tk_ref, tv_ref)):
                for stage in stages:
                    stage(u)
            return carry

        lax.fori_loop(0, n_chunks, step, 0)


def _hgrn_call(a_raw, lb_raw2, layer):
    n, t, _ = a_raw.shape
    tt = HGRN_TILE
    nt = t // tt
    fwd = lambda cb: pl.BlockSpec((1, tt, A_WIDTH), lambda b, i: (b, i, cb))
    bwd = lambda cb: pl.BlockSpec((1, tt, A_WIDTH), lambda b, i: (b, nt - 1 - i, cb))
    return pl.pallas_call(
        functools.partial(_hgrn_kernel, layer=layer, n_chunks=tt // GLA_CHUNK),
        grid=(n, nt),
        in_specs=[pl.BlockSpec(lb_raw2.shape, lambda b, i: (0, 0)),
                  fwd(0), fwd(1), fwd(3), bwd(0), bwd(2), bwd(3)],
        out_specs=[fwd(0), bwd(0)],
        out_shape=[jax.ShapeDtypeStruct((n, t, A_WIDTH), F32)] * 2,
        scratch_shapes=[pltpu.VMEM((A_WIDTH // LANES, LANES, LANES), F32)] * 4
        + [pltpu.VMEM((GLA_CHUNK, A_WIDTH), F32)] * 3,
        compiler_params=pltpu.CompilerParams(dimension_semantics=("arbitrary", "arbitrary"),
                                             vmem_limit_bytes=VMEM_LIMIT),
        name="hgrn",
    )(lb_raw2, a_raw, a_raw, a_raw, a_raw, a_raw, a_raw)


def _attn_kernel(*refs, w, seq_len, heads, use_sink, want_lse):
    q_ref, kp_ref, kc_ref, kn_ref, vp_ref, vc_ref, vn_ref = refs[:7]
    rest = list(refs[7:])
    sink_ref = rest.pop(0) if use_sink else None
    o_ref = rest.pop(0)
    lse_ref = rest.pop(0) if want_lse else None

    n_seq, tr = q_ref.shape[0], q_ref.shape[1]
    sb_rows, sb_keys = ATTN_SUB_ROWS, ATTN_SUB_ROWS + 2 * w
    n_sb = tr // sb_rows
    t0 = pl.program_id(1) * tr
    lane_half = lax.broadcasted_iota(jnp.int32, (1, LANES), 1) >> 6
    row = lax.broadcasted_iota(jnp.int32, (sb_rows, sb_keys), 0)
    col = lax.broadcasted_iota(jnp.int32, (sb_rows, sb_keys), 1)
    band = (col >= row) & (col - row <= 2 * w)

    kv_cache = {}

    def kv_variants(sq):
        if sq not in kv_cache:
            k_ext = jnp.concatenate([kp_ref[sq], kc_ref[sq], kn_ref[sq]], axis=0)
            v_ext = jnp.concatenate([vp_ref[sq], vc_ref[sq], vn_ref[sq]], axis=0)
            kv = {}
            for blk_heads in heads:
                for half, kv_blk in blk_heads:
                    if (half, kv_blk) not in kv:
                        sel = lane_half == half
                        kv[(half, kv_blk)] = (jnp.where(sel, k_ext[:, kv_blk * LANES:(kv_blk + 1) * LANES], 0),
                                              jnp.where(sel, v_ext[:, kv_blk * LANES:(kv_blk + 1) * LANES], 0))
            kv_cache[sq] = kv
        return kv_cache[sq]

    mask_cache = {}

    def tile_mask(sb):
        if sb not in mask_cache:
            mask = band
            if sb == 0 or sb == n_sb - 1:
                key_pos = col + (t0 - w + sb * sb_rows)
                mask = band & (key_pos >= 0) & (key_pos < seq_len)
            mask_cache[sb] = mask
        return mask_cache[sb]

    def stage_scores(u):
        sq, sb, j = u["id"]
        r0 = sb * sb_rows
        q = q_ref[sq, r0:r0 + sb_rows, j * LANES:(j + 1) * LANES]
        u["s"], u["m"], u["sink"] = [], [], []
        for e, (half, kv_blk) in enumerate(heads[j]):
            k_h = kv_variants(sq)[(half, kv_blk)][0]
            s = jnp.where(tile_mask(sb), _dot_nt(q, k_h[r0:r0 + sb_keys]), MASK_VALUE)
            m = jnp.max(s, axis=-1, keepdims=True)
            sink = None
            if use_sink:
                sink = sink_ref[2 * j + e] * LOG2_E
                m = jnp.maximum(m, sink)
            u["s"].append(s)
            u["m"].append(m)
            u["sink"].append(sink)

    def stage_probs(u):
        u["p"], u["den"] = [], []
        for s, m, sink in zip(u["s"], u["m"], u["sink"]):
            p = jnp.exp2(s - m)
            den = jnp.sum(p, axis=-1, keepdims=True)
            if use_sink:
                den = den + jnp.exp2(sink - m)
            u["p"].append(p.astype(BF16))
            u["den"].append(den)
        u["s"] = None

    def stage_out(u):
        sq, sb, j = u["id"]
        r0 = sb * sb_rows
        out = None
        lse_out = None
        for e, (half, kv_blk) in enumerate(heads[j]):
            v_h = kv_variants(sq)[(half, kv_blk)][1]
            o = _dot(u["p"][e], v_h[r0:r0 + sb_keys]) * (1.0 / u["den"][e])
            out = o if out is None else out + o
            if want_lse:
                lse = u["m"][e] + jnp.log2(u["den"][e])
                lse_out = lse if lse_out is None else jnp.where(lane_half == half, lse, lse_out)
        o_ref[sq, r0:r0 + sb_rows, j * LANES:(j + 1) * LANES] = out.astype(o_ref.dtype)
        if want_lse:
            lse_ref[sq, r0:r0 + sb_rows, :] = jnp.broadcast_to(lse_out, (sb_rows, LANES))

    units = [{"id": (sq, sb, j)} for sq in range(n_seq) for sb in range(n_sb) for j in range(len(heads))]
    skew = ATTN_STAGE_SKEW
    for i in range(len(units) + 2 * skew):
        if i < len(units):
            stage_scores(units[i])
        if 0 <= i - skew < len(units):
            stage_probs(units[i - skew])
        if 0 <= i - 2 * skew < len(units):
            stage_out(units[i - 2 * skew])


def _attn_call(q, k, v, sink, *, w, heads, want_lse, rows_per_step, name):
    n, seq_len, qw = q.shape
    kw = k.shape[-1]
    tr = min(rows_per_step, seq_len)
    nb = min(rows_per_step // tr, n)
    assert seq_len % tr == 0 and n % nb == 0 and tr % ATTN_SUB_ROWS == 0 and tr % w == 0
    per = tr // w
    last = seq_len // w - 1
    cur = lambda width: pl.BlockSpec((nb, tr, width), lambda b, i: (b, i, 0))
    prev = pl.BlockSpec((nb, w, kw), lambda b, i: (b, jnp.maximum(i * per - 1, 0), 0))
    nxt = pl.BlockSpec((nb, w, kw), lambda b, i: (b, jnp.minimum((i + 1) * per, last), 0))
    in_specs = [cur(qw), prev, cur(kw), nxt, prev, cur(kw), nxt]
    args = [q, k, k, k, v, v, v]
    if sink is not None:
        in_specs.append(pl.BlockSpec(memory_space=pltpu.SMEM))
        args.append(sink)
    out_shape = [jax.ShapeDtypeStruct((n, seq_len, qw), BF16)]
    out_specs = [cur(qw)]
    if want_lse:
        out_shape.append(jax.ShapeDtypeStruct((n, seq_len, LANES), F32))
        out_specs.append(cur(LANES))
    return pl.pallas_call(
        functools.partial(_attn_kernel, w=w, seq_len=seq_len, heads=heads, use_sink=sink is not None,
                          want_lse=want_lse),
        grid=(n // nb, seq_len // tr),
        in_specs=in_specs,
        out_specs=out_specs,
        out_shape=out_shape,
        compiler_params=pltpu.CompilerParams(dimension_semantics=("arbitrary", "arbitrary"),
                                             vmem_limit_bytes=VMEM_LIMIT),
        name=name,
    )(*args)


_B_HEADS = tuple(tuple((h % 2, 0 if (h // 3) == (h % 2) else 1) for h in (2 * j, 2 * j + 1)) for j in range(3))
_C_HEADS = (((0, 0), (1, 0)),)


def _outffn_kernel(x_ref, of_ref, ob_ref, gate_ref, ag_ref, yb_ref,
                   o1_ref, l1_ref, o4_ref, l4_ref, o16_ref, l16_ref,
                   wout_ref, n2_ref, wg_ref, wu_ref, wd_ref,
                   y_ref, s4o_ref, s4l_ref, s16o_ref, s16l_ref):
    tm = x_ref.shape[1]
    bd = _head_blockdiag2(LANES)
    ya = []
    for j in range(A_WIDTH // LANES):
        cols = slice(j * LANES, (j + 1) * LANES)
        o = of_ref[0, :, cols] + ob_ref[0, :, cols]
        ya.append(_head_rmsnorm(o, ag_ref[...], bd) * _silu(gate_ref[0, :, cols]))
    ya = jnp.concatenate(ya, axis=1).astype(BF16)

    def interleave(o_ref, l_ref, so_ref, sl_ref, d):
        for r in range(d):
            so_ref[pl.ds(r, tm // d, stride=d), :] = o_ref[0, r].astype(F32)
            sl_ref[pl.ds(r, tm // d, stride=d), :] = l_ref[0, r]
        return so_ref[...], sl_ref[...]

    o1, l1 = o1_ref[0].astype(F32), l1_ref[0]
    o4, l4 = interleave(o4_ref, l4_ref, s4o_ref, s4l_ref, 4)
    o16, l16 = interleave(o16_ref, l16_ref, s16o_ref, s16l_ref, 16)
    m = jnp.maximum(jnp.maximum(l1, l4), l16)
    e1, e4, e16 = jnp.exp2(l1 - m), jnp.exp2(l4 - m), jnp.exp2(l16 - m)
    yc = ((e1 * o1 + e4 * o4 + e16 * o16) / (e1 + e4 + e16)).astype(BF16)

    h = x_ref[0] + _dot(jnp.concatenate([ya, yb_ref[0], yc], axis=1), wout_ref[...])
    hn = (h * lax.rsqrt(jnp.mean(h * h, axis=-1, keepdims=True) + NORM_EPS) * n2_ref[...]).astype(BF16)

    y = h
    c0 = 0
    for width in FF_GROUPS:
        cols = slice(c0, c0 + width)
        act = (_silu(_dot(hn, wg_ref[:, cols])) * _dot(hn, wu_ref[:, cols])).astype(BF16)
        y = y + _dot(act, wd_ref[cols, :])
        c0 += width
    y_ref[0] = y


def _outffn_call(x, o_f, o_b, a_raw, ag, yb, c_out, w_out, n2, wg, wu, wd):
    n, t, _ = x.shape
    tm = TOKEN_TILE
    tok = lambda width, cb=0: pl.BlockSpec((1, tm, width), lambda b, i: (b, i, cb))
    dil = lambda d: pl.BlockSpec((1, d, tm // d, LANES), lambda b, i: (b, 0, i, 0))
    const = lambda shape: pl.BlockSpec(shape, lambda b, i: (0,) * len(shape), pipeline_mode=pl.Buffered(1))
    (o1, l1), (o4, l4), (o16, l16) = c_out
    return pl.pallas_call(
        _outffn_kernel,
        grid=(n, t // tm),
        in_specs=[tok(D_MODEL), tok(A_WIDTH), tok(A_WIDTH), tok(A_WIDTH, 4), const((1, LANES)), tok(3 * LANES),
                  tok(LANES), tok(LANES), dil(4), dil(4), dil(16), dil(16),
                  const(w_out.shape), const((1, D_MODEL)), const(wg.shape), const(wu.shape), const(wd.shape)],
        out_specs=tok(D_MODEL),
        out_shape=jax.ShapeDtypeStruct((n, t, D_MODEL), F32),
        scratch_shapes=[pltpu.VMEM((tm, LANES), F32)] * 4,
        compiler_params=pltpu.CompilerParams(dimension_semantics=("arbitrary", "arbitrary"),
                                             vmem_limit_bytes=VMEM_LIMIT),
        name="outffn",
    )(x, o_f, o_b, a_raw, ag, yb, o1, l1, o4, l4, o16, l16, w_out, n2, wg, wu, wd)


def _layer(x, p, layer, cos, sin):
    n, t, _ = x.shape
    outs = _proj_call(x, p["n1"], p["w_in"], cos, sin, p["bqg"], p["bkg"], p["cqg"], p["ckg"])
    a_raw, bq, bk, bv = outs[:4]
    o_f, o_b = _hgrn_call(a_raw, p["lb_raw"], layer)
    (yb,) = _attn_call(bq, bk, bv, p["sink"], w=B_HALF_WINDOW, heads=_B_HEADS, want_lse=False,
                       rows_per_step=ATTN_ROWS_WINDOW, name="attn_window")
    c_out = []
    for i, d in enumerate(C_DILATIONS):
        q, k, v = (a.reshape(n * d, t // d, LANES) for a in outs[4 + 3 * i:7 + 3 * i])
        o, lse = _attn_call(q, k, v, None, w=C_HALF_WINDOW, heads=_C_HEADS, want_lse=True,
                            rows_per_step=ATTN_ROWS_DILATED, name=f"attn_dil{d}")
        shape = (n, t, LANES) if d == 1 else (n, d, t // d, LANES)
        c_out.append((o.reshape(shape), lse.reshape(shape)))
    return _outffn_call(x, o_f, o_b, a_raw, p["ag"], yb, c_out, p["w_out"], p["n2"], p["wg"], p["wu"], p["wd"])


def _layer_params(l, norm1_g, w_in, lb_raw, a_norm_g, b_qn_g, b_kn_g, b_sink, c_qn_g, c_kn_g, w_out, norm2_g,
                  w_gate, w_up, w_down):
    pair = lambda g: jnp.tile(g[l].astype(F32), LANES // HEAD_DIM)[None, :]
    return {
        "n1": norm1_g[l].astype(F32)[None, :],
        "w_in": w_in[l].astype(BF16),
        "lb_raw": lb_raw.astype(F32).reshape(lb_raw.shape[0], 2 * A_WIDTH),
        "ag": pair(a_norm_g), "bqg": pair(b_qn_g), "bkg": pair(b_kn_g), "cqg": pair(c_qn_g), "ckg": pair(c_kn_g),
        "sink": b_sink[l].astype(F32),
        "w_out": w_out[l].astype(BF16),
        "n2": norm2_g[l].astype(F32)[None, :],
        "wg": w_gate[l].astype(BF16),
        "wu": w_up[l].astype(BF16),
        "wd": w_down[l].astype(BF16),
    }


def kernel(x_prompt, x_sample, norm1_g, w_in, lb_raw, a_norm_g, b_qn_g, b_kn_g, b_sink, c_qn_g, c_kn_g, w_out,
           norm2_g, w_gate, w_up, w_down):
    depth = w_in.shape[0]
    params = [_layer_params(l, norm1_g, w_in, lb_raw, a_norm_g, b_qn_g, b_kn_g, b_sink, c_qn_g, c_kn_g, w_out,
                            norm2_g, w_gate, w_up, w_down) for l in range(depth)]

    def trunk(x):
        cos, sin = _rope_tables(x.shape[1])
        for l in range(depth):
            x = _layer(x, params[l], l, cos, sin)
        return x

    return (trunk(x_prompt), trunk(x_sample))
```

```python
import functools

import jax
import jax.numpy as jnp
import numpy as np
from jax import lax
from jax.experimental import pallas as pl
from jax.experimental.pallas import tpu as pltpu

F32 = jnp.float32
BF16 = jnp.bfloat16

D_MODEL = 1024
HEAD_DIM = 64
LANES = 128
A_WIDTH = 256
A_COLS = 5 * A_WIDTH
BQ_COL, BK_COL, BV_COL = 1280, 1664, 1792
CQ_COL, CK_COL, CV_COL = 1920, 2304, 2688
B_HALF_WINDOW = 128
C_HALF_WINDOW = 64
C_DILATIONS = (1, 4, 16)
D_FF = 2816
FF_GROUPS = (1024, 1024, 768)
ROPE_THETA = 10000.0
NORM_EPS = 1e-6
MASK_VALUE = -1e30
GATE_FLOOR = 1e-30
GLA_CHUNK = 64
GLA_SAFE_RANGE = 110.0

TOKEN_TILE = 512
OUT_TOKEN_TILE = 512
HGRN_TILE = 1024
ATTN_ROWS_WINDOW = 2048
ATTN_ROWS_DILATED = 2048
HGRN_STAGE_SKEW = 2
ATTN_SUB_ROWS = 128
ATTN_STAGE_SKEW = 1
LOG2_E = 1.4426950408889634
VMEM_LIMIT = 56 * 1024 * 1024


def _dot(a, b):
    return jnp.dot(a, b, preferred_element_type=F32)


def _dot_nt(a, b):
    return lax.dot_general(a, b, (((1,), (1,)), ((), ())), preferred_element_type=F32)


def _dot_tn(a, b):
    return lax.dot_general(a, b, (((0,), (0,)), ((), ())), preferred_element_type=F32)


def _head_blockdiag(n):
    r = lax.broadcasted_iota(jnp.int32, (n, n), 0) >> 6
    c = lax.broadcasted_iota(jnp.int32, (n, n), 1) >> 6
    return (r == c).astype(BF16)


def _head_blockdiag2(n):
    bd = _head_blockdiag(n)
    return jnp.concatenate([bd, bd], axis=0)


def _head_sum(x, bd2):
    hi = x.astype(BF16)
    lo = (x - hi.astype(F32)).astype(BF16)
    return _dot(jnp.concatenate([hi, lo], axis=1), bd2)


def _head_rmsnorm(y, g, bd2):
    return y * lax.rsqrt(_head_sum(y * y, bd2) * (1.0 / HEAD_DIM) + NORM_EPS) * g


def _silu(x):
    half = 0.5 * x
    return half * jnp.tanh(half) + half


def _proj_kernel(x_ref, g1_ref, w_ref, cos_ref, sin_ref, bqg_ref, bkg_ref, cqg_ref, ckg_ref,
                 a_ref, bq_ref, bk_ref, bv_ref,
                 cq1_ref, ck1_ref, cv1_ref, cq4_ref, ck4_ref, cv4_ref, cq16_ref, ck16_ref, cv16_ref,
                 scr_ref):
    tm = x_ref.shape[1]
    x = x_ref[0]
    xn = (x * lax.rsqrt(jnp.mean(x * x, axis=-1, keepdims=True) + NORM_EPS) * g1_ref[...]).astype(BF16)
    y_bc = _dot(xn, w_ref[:, A_COLS:])

    bd = _head_blockdiag2(LANES)
    cos = cos_ref[...]
    sin = sin_ref[...]
    lane = lax.broadcasted_iota(jnp.int32, (1, LANES), 1)
    first_half = (lane & (HEAD_DIM - 1)) < (HEAD_DIM // 2)

    def proj(col):
        return y_bc[:, col - A_COLS:col - A_COLS + LANES]

    def norm_rope(col, g_ref):
        y = _head_rmsnorm(proj(col), g_ref[...], bd)
        partner = jnp.where(first_half, pltpu.roll(y, LANES - HEAD_DIM // 2, 1), pltpu.roll(y, HEAD_DIM // 2, 1))
        return y * cos + partner * sin

    scale = LOG2_E * HEAD_DIM ** -0.5
    for j in range(3):
        bq_ref[0, :, j * LANES:(j + 1) * LANES] = (norm_rope(BQ_COL + j * LANES, bqg_ref) * scale).astype(BF16)
    bk = norm_rope(BK_COL, bkg_ref)
    bk_ref[0, :, 0:LANES] = bk.astype(BF16)
    bk_ref[0, :, LANES:2 * LANES] = pltpu.roll(bk, HEAD_DIM, 1).astype(BF16)
    bv = proj(BV_COL)
    bv_ref[0, :, 0:LANES] = bv.astype(BF16)
    bv_ref[0, :, LANES:2 * LANES] = pltpu.roll(bv, HEAD_DIM, 1).astype(BF16)

    def store_dilated(ref, y, d, slot):
        if d == 1:
            ref[0] = y.astype(BF16)
            return
        scr_ref[slot] = y
        for r in range(d):
            ref[0, r] = scr_ref[slot, pl.ds(r, tm // d, stride=d), :].astype(BF16)

    c_refs = ((cq1_ref, ck1_ref, cv1_ref), (cq4_ref, ck4_ref, cv4_ref), (cq16_ref, ck16_ref, cv16_ref))
    for p, d in enumerate(C_DILATIONS):
        q_ref, k_ref, v_ref = c_refs[p]
        store_dilated(q_ref, norm_rope(CQ_COL + p * LANES, cqg_ref) * scale, d, 3 * p)
        store_dilated(k_ref, norm_rope(CK_COL + p * LANES, ckg_ref), d, 3 * p + 1)
        store_dilated(v_ref, proj(CV_COL + p * LANES), d, 3 * p + 2)

    a_ref[0] = _dot(xn, w_ref[:, 0:A_COLS])


def _rope_tables(t_len):
    half = HEAD_DIM // 2
    inv = ROPE_THETA ** (-jnp.arange(half, dtype=F32) / half)
    ang = jnp.arange(t_len).astype(F32)[:, None] * inv[None, :]
    cos = jnp.tile(jnp.cos(ang), (1, 2 * LANES // HEAD_DIM))
    sin = jnp.sin(ang)
    sin = jnp.tile(jnp.concatenate([-sin, sin], axis=1), (1, LANES // HEAD_DIM))
    return cos, sin


def _proj_call(x, g1, w_in, cos, sin, bqg, bkg, cqg, ckg):
    n, t, _ = x.shape
    tm = TOKEN_TILE
    grid = (n, t // tm)
    tok = lambda width: pl.BlockSpec((1, tm, width), lambda b, i: (b, i, 0))
    const = lambda shape: pl.BlockSpec(shape, lambda b, i: (0,) * len(shape))
    dil = lambda d: pl.BlockSpec((1, d, tm // d, LANES), lambda b, i: (b, 0, i, 0))
    out_shape = [jax.ShapeDtypeStruct((n, t, A_COLS), F32),
                 jax.ShapeDtypeStruct((n, t, 3 * LANES), BF16),
                 jax.ShapeDtypeStruct((n, t, 2 * LANES), BF16),
                 jax.ShapeDtypeStruct((n, t, 2 * LANES), BF16)]
    out_specs = [tok(A_COLS), tok(3 * LANES), tok(2 * LANES), tok(2 * LANES)]
    for d in C_DILATIONS:
        for _ in range(3):
            if d == 1:
                out_shape.append(jax.ShapeDtypeStruct((n, t, LANES), BF16))
                out_specs.append(tok(LANES))
            else:
                out_shape.append(jax.ShapeDtypeStruct((n, d, t // d, LANES), BF16))
                out_specs.append(dil(d))
    rope_spec = pl.BlockSpec((tm, LANES), lambda b, i: (i, 0))
    return pl.pallas_call(
        _proj_kernel,
        grid=grid,
        in_specs=[tok(D_MODEL), const((1, D_MODEL)), const(w_in.shape), rope_spec, rope_spec,
                  const((1, LANES)), const((1, LANES)), const((1, LANES)), const((1, LANES))],
        out_specs=out_specs,
        out_shape=out_shape,
        scratch_shapes=[pltpu.VMEM((3 * len(C_DILATIONS), tm, LANES), F32)],
        compiler_params=pltpu.CompilerParams(dimension_semantics=("arbitrary", "arbitrary"),
                                             vmem_limit_bytes=VMEM_LIMIT),
        name="proj",
    )(x, g1, w_in, cos, sin, bqg, bkg, cqg, ckg)


def _hgrn_gates(u):
    c = GLA_CHUNK
    reverse, lb = u["reverse"], u["lb"]
    q, f_raw, v = u["load"]()
    half_t = 0.5 * jnp.tanh(0.5 * f_raw)
    f = lb + (1.0 - lb) * (0.5 + half_t)
    g = jnp.log2(jnp.maximum(f, GATE_FLOOR))
    kk = (1.0 - lb) * (0.5 - half_t)

    row = lax.broadcasted_iota(jnp.int32, (c, 2 * c), 0)
    col = lax.broadcasted_iota(jnp.int32, (c, 2 * c), 1) & (c - 1)
    tri2 = ((col >= row) if reverse else (col <= row)).astype(BF16)
    g1 = g.astype(BF16)
    g2 = (g - g1.astype(F32)).astype(BF16)
    b = _dot(tri2, jnp.concatenate([g1, g2], axis=0))
    u.update(q=q, v=v, kk=kk, b=b)


def _hgrn_state(u):
    c = GLA_CHUNK
    reverse, state_ref = u["reverse"], u["state_ref"]
    q, v, kk, b = u["q"], u["v"], u["kk"], u["b"]
    total = b[0:1] if reverse else b[c - 1:c]
    mid = b[c // 2:c // 2 + 1] if reverse else b[c // 2 - 1:c // 2]
    v16 = v.astype(BF16)

    qs = (q * jnp.exp2(b)).astype(BF16)
    ks = (kk * jnp.exp2(total - b)).astype(BF16)
    decay = jnp.exp2(total)
    srow = lax.broadcasted_iota(jnp.int32, (LANES, LANES), 0) >> 6
    scol = lax.broadcasted_iota(jnp.int32, (LANES, LANES), 1) >> 6
    same_head = srow == scol
    o = []
    for j in range(A_WIDTH // LANES):
        cols = slice(j * LANES, (j + 1) * LANES)
        state = state_ref[j]
        o.append(_dot_nt(qs[:, cols], state.astype(BF16)))
        upd = _dot_tn(v16[:, cols], ks[:, cols])
        state_ref[j] = state * decay[:, cols] + jnp.where(same_head, upd, 0.0)

    offset = jnp.maximum(jnp.abs(b[0:1] - mid), jnp.abs(b[c - 1:c] - mid))
    u.update(o=o, v16=v16, offset=offset)

    if u["exact_refs"] is None:
        qp = q * jnp.exp2(b - mid)
        kp = (kk * jnp.exp2(mid - b)).astype(BF16)
        lane_half = lax.broadcasted_iota(jnp.int32, (1, LANES), 1) >> 6
        t_idx = lax.broadcasted_iota(jnp.int32, (2 * c, c), 0) & (c - 1)
        s_idx = lax.broadcasted_iota(jnp.int32, (2 * c, c), 1)
        causal = (s_idx >= t_idx) if reverse else (s_idx <= t_idx)
        u["a16"] = []
        for j in range(A_WIDTH // LANES):
            cols = slice(j * LANES, (j + 1) * LANES)
            q_stack = jnp.concatenate([jnp.where(lane_half == h, qp[:, cols], 0.0) for h in range(2)],
                                      axis=0).astype(BF16)
            a = _dot_nt(q_stack, kp[:, cols])
            u["a16"].append(jnp.where(causal, a, 0.0).astype(BF16))


def _hgrn_output(u):
    c = GLA_CHUNK
    reverse = u["reverse"]
    if u["exact_refs"] is None:
        lane_half = lax.broadcasted_iota(jnp.int32, (1, LANES), 1) >> 6
        o = []
        for j in range(A_WIDTH // LANES):
            v_blk = u["v16"][:, j * LANES:(j + 1) * LANES]
            acc = u["o"][j]
            for h in range(2):
                acc = acc + _dot(u["a16"][j][h * c:(h + 1) * c], jnp.where(lane_half == h, v_blk, 0))
            o.append(acc)
        o = jnp.concatenate(o, axis=1)
    else:
        q, b = u["q"], u["b"]
        tb_ref, tk_ref, tv_ref = u["exact_refs"]
        tb_ref[...] = b
        tk_ref[...] = u["kk"]
        tv_ref[...] = u["v"]
        bd = _head_blockdiag(A_WIDTH)
        t_col = lax.broadcasted_iota(jnp.int32, (c, 1), 0)

        def body(si, acc):
            b_s = tb_ref[pl.ds(si, 1), :]
            dec = jnp.exp2(jnp.minimum(b - b_s, 0.0))
            valid = (t_col <= si) if reverse else (t_col >= si)
            p = jnp.where(valid, q * dec * tk_ref[pl.ds(si, 1), :], 0.0)
            return acc + _dot(p.astype(BF16), bd) * tv_ref[pl.ds(si, 1), :]

        o = lax.fori_loop(0, c, body, jnp.concatenate(u["o"], axis=1))
    u["store"](o)


def _hgrn_kernel(lb_ref, qf_ref, ff_ref, vf_ref, qb_ref, fb_ref, vb_ref, of_ref, ob_ref,
                 sf_ref, sb_ref, sf0_ref, sb0_ref, tb_ref, tk_ref, tv_ref, *, layer, n_chunks):
    @pl.when(pl.program_id(1) == 0)
    def _():
        sf_ref[...] = jnp.zeros_like(sf_ref)
        sb_ref[...] = jnp.zeros_like(sb_ref)

    sf0_ref[...] = sf_ref[...]
    sb0_ref[...] = sb_ref[...]

    raw = lb_ref[...]
    depth = raw.shape[0]
    mx = raw[0:1]
    for i in range(1, depth):
        mx = jnp.maximum(mx, raw[i:i + 1])
    e = jnp.exp(raw - mx)
    den = e[0:1]
    for i in range(1, depth):
        den = den + e[i:i + 1]
    sm = e / den
    cs = sm[0:1]
    for i in range(1, layer + 1):
        cs = cs + sm[i:i + 1]
    lb = cs - sm[0:1]
    lb_f = lb[:, 0:A_WIDTH]
    lb_b = lb[:, A_WIDTH:2 * A_WIDTH]

    def chunk_rows(ci):
        start = ci * GLA_CHUNK
        return pl.ds(start if isinstance(ci, int) else pl.multiple_of(start, GLA_CHUNK), GLA_CHUNK)

    def chunk_pair(ci, exact_refs):
        def record(q_ref, f_ref, v_ref, o_ref, rows, lb_d, state_ref, reverse):
            def store(o):
                o_ref[0, rows, :] = o
            return {"load": lambda: (q_ref[0, rows, :], f_ref[0, rows, :], v_ref[0, rows, :]), "store": store,
                    "lb": lb_d, "state_ref": state_ref, "reverse": reverse, "exact_refs": exact_refs}
        return [record(qf_ref, ff_ref, vf_ref, of_ref, chunk_rows(ci), lb_f, sf_ref, False),
                record(qb_ref, fb_ref, vb_ref, ob_ref, chunk_rows(n_chunks - 1 - ci), lb_b, sb_ref, True)]

    units = [u for ci in range(n_chunks) for u in chunk_pair(ci, None)]
    stages = (_hgrn_gates, _hgrn_state, _hgrn_output)
    skew = HGRN_STAGE_SKEW
    for i in range(len(units) + skew * (len(stages) - 1)):
        for k, stage in enumerate(stages):
            if 0 <= i - k * skew < len(units):
                stage(units[i - k * skew])
    worst = units[0]["offset"]
    for u in units[1:]:
        worst = jnp.maximum(worst, u["offset"])

    @pl.when(jnp.max(worst) > GLA_SAFE_RANGE)
    def _():
        sf_ref[...] = sf0_ref[...]
        sb_ref[...] = sb0_ref[...]

        def step(ci, carry):
            for u in chunk_pair(ci, (tb_ref, tk_ref, tv_ref)):
                for stage in stages:
                    stage(u)
            return carry

        lax.fori_loop(0, n_chunks, step, 0)


def _hgrn_call(a_raw, lb_raw2, layer):
    n, t, _ = a_raw.shape
    tt = HGRN_TILE
    nt = t // tt
    fwd = lambda cb: pl.BlockSpec((1, tt, A_WIDTH), lambda b, i: (b, i, cb))
    bwd = lambda cb: pl.BlockSpec((1, tt, A_WIDTH), lambda b, i: (b, nt - 1 - i, cb))
    return pl.pallas_call(
        functools.partial(_hgrn_kernel, layer=layer, n_chunks=tt // GLA_CHUNK),
        grid=(n, nt),
        in_specs=[pl.BlockSpec(lb_raw2.shape, lambda b, i: (0, 0)),
                  fwd(0), fwd(1), fwd(3), bwd(0), bwd(2), bwd(3)],
        out_specs=[fwd(0), bwd(0)],
        out_shape=[jax.ShapeDtypeStruct((n, t, A_WIDTH), F32)] * 2,
        scratch_shapes=[pltpu.VMEM((A_WIDTH // LANES, LANES, LANES), F32)] * 4
        + [pltpu.VMEM((GLA_CHUNK, A_WIDTH), F32)] * 3,
        compiler_params=pltpu.CompilerParams(dimension_semantics=("arbitrary", "arbitrary"),
                                             vmem_limit_bytes=VMEM_LIMIT),
        name="hgrn",
    )(lb_raw2, a_raw, a_raw, a_raw, a_raw, a_raw, a_raw)


def _attn_kernel(*refs, w, seq_len, heads, use_sink, want_lse):
    q_ref, kp_ref, kc_ref, kn_ref, vp_ref, vc_ref, vn_ref = refs[:7]
    rest = list(refs[7:])
    sink_ref = rest.pop(0) if use_sink else None
    o_ref = rest.pop(0)
    lse_ref = rest.pop(0) if want_lse else None

    n_seq, tr = q_ref.shape[0], q_ref.shape[1]
    sb_rows, sb_keys = ATTN_SUB_ROWS, ATTN_SUB_ROWS + 2 * w
    n_sb = tr // sb_rows
    t0 = pl.program_id(1) * tr
    lane_half = lax.broadcasted_iota(jnp.int32, (1, LANES), 1) >> 6
    row = lax.broadcasted_iota(jnp.int32, (sb_rows, sb_keys), 0)
    col = lax.broadcasted_iota(jnp.int32, (sb_rows, sb_keys), 1)
    band = (col >= row) & (col - row <= 2 * w)

    kv_cache = {}

    def kv_variants(sq):
        if sq not in kv_cache:
            k_ext = jnp.concatenate([kp_ref[sq], kc_ref[sq], kn_ref[sq]], axis=0)
            v_ext = jnp.concatenate([vp_ref[sq], vc_ref[sq], vn_ref[sq]], axis=0)
            kv = {}
            for blk_heads in heads:
                for half, kv_blk in blk_heads:
                    if (half, kv_blk) not in kv:
                        sel = lane_half == half
                        kv[(half, kv_blk)] = (jnp.where(sel, k_ext[:, kv_blk * LANES:(kv_blk + 1) * LANES], 0),
                                              jnp.where(sel, v_ext[:, kv_blk * LANES:(kv_blk + 1) * LANES], 0))
            kv_cache[sq] = kv
        return kv_cache[sq]

    mask_cache = {}

    def tile_mask(sb):
        if sb not in mask_cache:
            mask = band
            if sb == 0 or sb == n_sb - 1:
                key_pos = col + (t0 - w + sb * sb_rows)
                mask = band & (key_pos >= 0) & (key_pos < seq_len)
            mask_cache[sb] = mask
        return mask_cache[sb]

    def stage_scores(u):
        sq, sb, j = u["id"]
        r0 = sb * sb_rows
        q = q_ref[sq, r0:r0 + sb_rows, j * LANES:(j + 1) * LANES]
        u["s"], u["m"], u["sink"] = [], [], []
        for e, (half, kv_blk) in enumerate(heads[j]):
            k_h = kv_variants(sq)[(half, kv_blk)][0]
            s = jnp.where(tile_mask(sb), _dot_nt(q, k_h[r0:r0 + sb_keys]), MASK_VALUE)
            m = jnp.max(s, axis=-1, keepdims=True)
            sink = None
            if use_sink:
                sink = sink_ref[2 * j + e] * LOG2_E
                m = jnp.maximum(m, sink)
            u["s"].append(s)
            u["m"].append(m)
            u["sink"].append(sink)

    def stage_probs(u):
        u["p"], u["den"] = [], []
        for s, m, sink in zip(u["s"], u["m"], u["sink"]):
            p = jnp.exp2(s - m)
            den = jnp.sum(p, axis=-1, keepdims=True)
            if use_sink:
                den = den + jnp.exp2(sink - m)
            u["p"].append(p.astype(BF16))
            u["den"].append(den)
        u["s"] = None

    def stage_out(u):
        sq, sb, j = u["id"]
        r0 = sb * sb_rows
        out = None
        lse_out = None
        for e, (half, kv_blk) in enumerate(heads[j]):
            v_h = kv_variants(sq)[(half, kv_blk)][1]
            o = _dot(u["p"][e], v_h[r0:r0 + sb_keys]) * (1.0 / u["den"][e])
            out = o if out is None else out + o
            if want_lse:
                lse = u["m"][e] + jnp.log2(u["den"][e])
                lse_out = lse if lse_out is None else jnp.where(lane_half == half, lse, lse_out)
        o_ref[sq, r0:r0 + sb_rows, j * LANES:(j + 1) * LANES] = out.astype(o_ref.dtype)
        if want_lse:
            lse_ref[sq, r0:r0 + sb_rows, :] = jnp.broadcast_to(lse_out, (sb_rows, LANES))

    units = [{"id": (sq, sb, j)} for sq in range(n_seq) for sb in range(n_sb) for j in range(len(heads))]
    skew = ATTN_STAGE_SKEW
    for i in range(len(units) + 2 * skew):
        if i < len(units):
            stage_scores(units[i])
        if 0 <= i - skew < len(units):
            stage_probs(units[i - skew])
        if 0 <= i - 2 * skew < len(units):
            stage_out(units[i - 2 * skew])


def _attn_call(q, k, v, sink, *, w, heads, want_lse, rows_per_step, name):
    n, seq_len, qw = q.shape
    kw = k.shape[-1]
    tr = min(rows_per_step, seq_len)
    nb = min(rows_per_step // tr, n)
    assert seq_len % tr == 0 and n % nb == 0 and tr % ATTN_SUB_ROWS == 0 and tr % w == 0
    per = tr // w
    last = seq_len // w - 1
    cur = lambda width: pl.BlockSpec((nb, tr, width), lambda b, i: (b, i, 0))
    prev = pl.BlockSpec((nb, w, kw), lambda b, i: (b, jnp.maximum(i * per - 1, 0), 0))
    nxt = pl.BlockSpec((nb, w, kw), lambda b, i: (b, jnp.minimum((i + 1) * per, last), 0))
    in_specs = [cur(qw), prev, cur(kw), nxt, prev, cur(kw), nxt]
    args = [q, k, k, k, v, v, v]
    if sink is not None:
        in_specs.append(pl.BlockSpec(memory_space=pltpu.SMEM))
        args.append(sink)
    out_shape = [jax.ShapeDtypeStruct((n, seq_len, qw), BF16)]
    out_specs = [cur(qw)]
    if want_lse:
        out_shape.append(jax.ShapeDtypeStruct((n, seq_len, LANES), F32))
        out_specs.append(cur(LANES))
    return pl.pallas_call(
        functools.partial(_attn_kernel, w=w, seq_len=seq_len, heads=heads, use_sink=sink is not None,
                          want_lse=want_lse),
        grid=(n // nb, seq_len // tr),
        in_specs=in_specs,
        out_specs=out_specs,
        out_shape=out_shape,
        compiler_params=pltpu.CompilerParams(dimension_semantics=("arbitrary", "arbitrary"),
                                             vmem_limit_bytes=VMEM_LIMIT),
        name=name,
    )(*args)


_B_HEADS = tuple(tuple((h % 2, 0 if (h // 3) == (h % 2) else 1) for h in (2 * j, 2 * j + 1)) for j in range(3))
_C_HEADS = (((0, 0), (1, 0)),)


def _outffn_kernel(x_ref, of_ref, ob_ref, gate_ref, ag_ref, yb_ref,
                   o1_ref, l1_ref, o4_ref, l4_ref, o16_ref, l16_ref,
                   wout_ref, n2_ref, wg_ref, wu_ref, wd_ref,
                   y_ref, s4o_ref, s4l_ref, s16o_ref, s16l_ref):
    tm = x_ref.shape[1]
    bd = _head_blockdiag2(LANES)
    ya = []
    for j in range(A_WIDTH // LANES):
        cols = slice(j * LANES, (j + 1) * LANES)
        o = of_ref[0, :, cols] + ob_ref[0, :, cols]
        ya.append(_head_rmsnorm(o, ag_ref[...], bd) * _silu(gate_ref[0, :, cols]))
    ya = jnp.concatenate(ya, axis=1).astype(BF16)

    def interleave(o_ref, l_ref, so_ref, sl_ref, d):
        for r in range(d):
            so_ref[pl.ds(r, tm // d, stride=d), :] = o_ref[0, r].astype(F32)
            sl_ref[pl.ds(r, tm // d, stride=d), :] = l_ref[0, r]
        return so_ref[...], sl_ref[...]

    o1, l1 = o1_ref[0].astype(F32), l1_ref[0]
    o4, l4 = interleave(o4_ref, l4_ref, s4o_ref, s4l_ref, 4)
    o16, l16 = interleave(o16_ref, l16_ref, s16o_ref, s16l_ref, 16)
    m = jnp.maximum(jnp.maximum(l1, l4), l16)
    e1, e4, e16 = jnp.exp2(l1 - m), jnp.exp2(l4 - m), jnp.exp2(l16 - m)
    yc = ((e1 * o1 + e4 * o4 + e16 * o16) / (e1 + e4 + e16)).astype(BF16)

    h = x_ref[0] + _dot(jnp.concatenate([ya, yb_ref[0], yc], axis=1), wout_ref[...])
    hn = (h * lax.rsqrt(jnp.mean(h * h, axis=-1, keepdims=True) + NORM_EPS) * n2_ref[...]).astype(BF16)

    y = h
    c0 = 0
    for width in FF_GROUPS:
        cols = slice(c0, c0 + width)
        act = (_silu(_dot(hn, wg_ref[:, cols])) * _dot(hn, wu_ref[:, cols])).astype(BF16)
        y = y + _dot(act, wd_ref[cols, :])
        c0 += width
    y_ref[0] = y


def _outffn_call(x, o_f, o_b, a_raw, ag, yb, c_out, w_out, n2, wg, wu, wd):
    n, t, _ = x.shape
    tm = OUT_TOKEN_TILE
    tok = lambda width, cb=0: pl.BlockSpec((1, tm, width), lambda b, i: (b, i, cb))
    dil = lambda d: pl.BlockSpec((1, d, tm // d, LANES), lambda b, i: (b, 0, i, 0))
    const = lambda shape: pl.BlockSpec(shape, lambda b, i: (0,) * len(shape), pipeline_mode=pl.Buffered(1))
    (o1, l1), (o4, l4), (o16, l16) = c_out
    return pl.pallas_call(
        _outffn_kernel,
        grid=(n, t // tm),
        in_specs=[tok(D_MODEL), tok(A_WIDTH), tok(A_WIDTH), tok(A_WIDTH, 4), const((1, LANES)), tok(3 * LANES),
                  tok(LANES), tok(LANES), dil(4), dil(4), dil(16), dil(16),
                  const(w_out.shape), const((1, D_MODEL)), const(wg.shape), const(wu.shape), const(wd.shape)],
        out_specs=tok(D_MODEL),
        out_shape=jax.ShapeDtypeStruct((n, t, D_MODEL), F32),
        scratch_shapes=[pltpu.VMEM((tm, LANES), F32)] * 4,
        compiler_params=pltpu.CompilerParams(dimension_semantics=("arbitrary", "arbitrary"),
                                             vmem_limit_bytes=VMEM_LIMIT),
        name="outffn",
    )(x, o_f, o_b, a_raw, ag, yb, o1, l1, o4, l4, o16, l16, w_out, n2, wg, wu, wd)


def _layer(x, p, layer, cos, sin):
    n, t, _ = x.shape
    outs = _proj_call(x, p["n1"], p["w_in"], cos, sin, p["bqg"], p["bkg"], p["cqg"], p["ckg"])
    a_raw, bq, bk, bv = outs[:4]
    o_f, o_b = _hgrn_call(a_raw, p["lb_raw"], layer)
    (yb,) = _attn_call(bq, bk, bv, p["sink"], w=B_HALF_WINDOW, heads=_B_HEADS, want_lse=False,
                       rows_per_step=ATTN_ROWS_WINDOW, name="attn_window")
    c_out = []
    for i, d in enumerate(C_DILATIONS):
        q, k, v = (a.reshape(n * d, t // d, LANES) for a in outs[4 + 3 * i:7 + 3 * i])
        o, lse = _attn_call(q, k, v, None, w=C_HALF_WINDOW, heads=_C_HEADS, want_lse=True,
                            rows_per_step=ATTN_ROWS_DILATED, name=f"attn_dil{d}")
        shape = (n, t, LANES) if d == 1 else (n, d, t // d, LANES)
        c_out.append((o.reshape(shape), lse.reshape(shape)))
    return _outffn_call(x, o_f, o_b, a_raw, p["ag"], yb, c_out, p["w_out"], p["n2"], p["wg"], p["wu"], p["wd"])


def _layer_params(l, norm1_g, w_in, lb_raw, a_norm_g, b_qn_g, b_kn_g, b_sink, c_qn_g, c_kn_g, w_out, norm2_g,
                  w_gate, w_up, w_down):
    pair = lambda g: jnp.tile(g[l].astype(F32), LANES // HEAD_DIM)[None, :]
    return {
        "n1": norm1_g[l].astype(F32)[None, :],
        "w_in": w_in[l].astype(BF16),
        "lb_raw": lb_raw.astype(F32).reshape(lb_raw.shape[0], 2 * A_WIDTH),
        "ag": pair(a_norm_g), "bqg": pair(b_qn_g), "bkg": pair(b_kn_g), "cqg": pair(c_qn_g), "ckg": pair(c_kn_g),
        "sink": b_sink[l].astype(F32),
        "w_out": w_out[l].astype(BF16),
        "n2": norm2_g[l].astype(F32)[None, :],
        "wg": w_gate[l].astype(BF16),
        "wu": w_up[l].astype(BF16),
        "wd": w_down[l].astype(BF16),
    }


def kernel(x_prompt, x_sample, norm1_g, w_in, lb_raw, a_norm_g, b_qn_g, b_kn_g, b_sink, c_qn_g, c_kn_g, w_out,
           norm2_g, w_gate, w_up, w_down):
    depth = w_in.shape[0]
    params = [_layer_params(l, norm1_g, w_in, lb_raw, a_norm_g, b_qn_g, b_kn_g, b_sink, c_qn_g, c_kn_g, w_out,
                            norm2_g, w_gate, w_up, w_down) for l in range(depth)]

    def trunk(x):
        cos, sin = _rope_tables(x.shape[1])
        for l in range(depth):
            x = _layer(x, params[l], l, cos, sin)
        return x

    return (trunk(x_prompt), trunk(x_sample))
```

```python
import functools

import jax
import jax.numpy as jnp
import numpy as np
from jax import lax
from jax.experimental import pallas as pl
from jax.experimental.pallas import tpu as pltpu

F32 = jnp.float32
BF16 = jnp.bfloat16

D_MODEL = 1024
HEAD_DIM = 64
LANES = 128
A_WIDTH = 256
A_COLS = 5 * A_WIDTH
BQ_COL, BK_COL, BV_COL = 1280, 1664, 1792
CQ_COL, CK_COL, CV_COL = 1920, 2304, 2688
B_HALF_WINDOW = 128
C_HALF_WINDOW = 64
C_DILATIONS = (1, 4, 16)
D_FF = 2816
FF_GROUPS = (1024, 1024, 768)
ROPE_THETA = 10000.0
NORM_EPS = 1e-6
MASK_VALUE = -1e30
GATE_FLOOR = 1e-30
GLA_CHUNK = 64
GLA_SAFE_RANGE = 110.0

TOKEN_TILE = 1024
OUT_TOKEN_TILE = 512
HGRN_TILE = 1024
ATTN_ROWS_WINDOW = 2048
ATTN_ROWS_DILATED = 2048
HGRN_STAGE_SKEW = 2
ATTN_SUB_ROWS = 128
ATTN_STAGE_SKEW = 1
LOG2_E = 1.4426950408889634
VMEM_LIMIT = 56 * 1024 * 1024


def _dot(a, b):
    return jnp.dot(a, b, preferred_element_type=F32)


def _dot_nt(a, b):
    return lax.dot_general(a, b, (((1,), (1,)), ((), ())), preferred_element_type=F32)


def _dot_tn(a, b):
    return lax.dot_general(a, b, (((0,), (0,)), ((), ())), preferred_element_type=F32)


def _head_blockdiag(n):
    r = lax.broadcasted_iota(jnp.int32, (n, n), 0) >> 6
    c = lax.broadcasted_iota(jnp.int32, (n, n), 1) >> 6
    return (r == c).astype(BF16)


def _head_mean_matrix(n):
    bd = _head_blockdiag(n) * (1.0 / HEAD_DIM)
    return jnp.concatenate([bd, bd], axis=0).astype(BF16)


def _head_mean(x, mean2):
    hi = x.astype(BF16)
    lo = (x - hi.astype(F32)).astype(BF16)
    return _dot(jnp.concatenate([hi, lo], axis=1), mean2)


def _head_rmsnorm(y, g, mean2):
    return y * lax.rsqrt(_head_mean(y * y, mean2) + NORM_EPS) * g


def _silu(x):
    half = 0.5 * x
    return half * jnp.tanh(half) + half


def _proj_kernel(x_ref, g1_ref, w_ref, cos_ref, sin_ref, bqg_ref, bkg_ref, cqg_ref, ckg_ref,
                 a_ref, bq_ref, bk_ref, bv_ref,
                 cq1_ref, ck1_ref, cv1_ref, cq4_ref, ck4_ref, cv4_ref, cq16_ref, ck16_ref, cv16_ref,
                 scr_ref):
    tm = x_ref.shape[1]
    x = x_ref[0]
    xn = (x * lax.rsqrt(jnp.mean(x * x, axis=-1, keepdims=True) + NORM_EPS) * g1_ref[...]).astype(BF16)
    y_bc = _dot(xn, w_ref[:, A_COLS:])

    bd = _head_mean_matrix(LANES)
    cos = cos_ref[...]
    sin = sin_ref[...]
    lane = lax.broadcasted_iota(jnp.int32, (1, LANES), 1)
    first_half = (lane & (HEAD_DIM - 1)) < (HEAD_DIM // 2)

    def proj(col):
        return y_bc[:, col - A_COLS:col - A_COLS + LANES]

    def norm_rope(col, gain):
        y = _head_rmsnorm(proj(col), gain, bd)
        partner = jnp.where(first_half, pltpu.roll(y, LANES - HEAD_DIM // 2, 1), pltpu.roll(y, HEAD_DIM // 2, 1))
        return y * cos + partner * sin

    scale = LOG2_E * HEAD_DIM ** -0.5
    bq_gain, bk_gain = bqg_ref[...] * scale, bkg_ref[...]
    cq_gain, ck_gain = cqg_ref[...] * scale, ckg_ref[...]
    for j in range(3):
        bq_ref[0, :, j * LANES:(j + 1) * LANES] = norm_rope(BQ_COL + j * LANES, bq_gain).astype(BF16)
    bk = norm_rope(BK_COL, bk_gain)
    bk_ref[0, :, 0:LANES] = bk.astype(BF16)
    bk_ref[0, :, LANES:2 * LANES] = pltpu.roll(bk, HEAD_DIM, 1).astype(BF16)
    bv = proj(BV_COL)
    bv_ref[0, :, 0:LANES] = bv.astype(BF16)
    bv_ref[0, :, LANES:2 * LANES] = pltpu.roll(bv, HEAD_DIM, 1).astype(BF16)

    def store_dilated(ref, y, d, slot):
        if d == 1:
            ref[0] = y.astype(BF16)
            return
        scr_ref[slot] = y
        for r in range(d):
            ref[0, r] = scr_ref[slot, pl.ds(r, tm // d, stride=d), :].astype(BF16)

    c_refs = ((cq1_ref, ck1_ref, cv1_ref), (cq4_ref, ck4_ref, cv4_ref), (cq16_ref, ck16_ref, cv16_ref))
    for p, d in enumerate(C_DILATIONS):
        q_ref, k_ref, v_ref = c_refs[p]
        store_dilated(q_ref, norm_rope(CQ_COL + p * LANES, cq_gain), d, 3 * p)
        store_dilated(k_ref, norm_rope(CK_COL + p * LANES, ck_gain), d, 3 * p + 1)
        store_dilated(v_ref, proj(CV_COL + p * LANES), d, 3 * p + 2)

    a_ref[0] = _dot(xn, w_ref[:, 0:A_COLS])


def _rope_tables(t_len):
    half = HEAD_DIM // 2
    inv = ROPE_THETA ** (-jnp.arange(half, dtype=F32) / half)
    ang = jnp.arange(t_len).astype(F32)[:, None] * inv[None, :]
    cos = jnp.tile(jnp.cos(ang), (1, 2 * LANES // HEAD_DIM))
    sin = jnp.sin(ang)
    sin = jnp.tile(jnp.concatenate([-sin, sin], axis=1), (1, LANES // HEAD_DIM))
    return cos, sin


def _proj_call(x, g1, w_in, cos, sin, bqg, bkg, cqg, ckg):
    n, t, _ = x.shape
    tm = TOKEN_TILE
    grid = (n, t // tm)
    tok = lambda width: pl.BlockSpec((1, tm, width), lambda b, i: (b, i, 0))
    const = lambda shape: pl.BlockSpec(shape, lambda b, i: (0,) * len(shape), pipeline_mode=pl.Buffered(1))
    dil = lambda d: pl.BlockSpec((1, d, tm // d, LANES), lambda b, i: (b, 0, i, 0))
    out_shape = [jax.ShapeDtypeStruct((n, t, A_COLS), F32),
                 jax.ShapeDtypeStruct((n, t, 3 * LANES), BF16),
                 jax.ShapeDtypeStruct((n, t, 2 * LANES), BF16),
                 jax.ShapeDtypeStruct((n, t, 2 * LANES), BF16)]
    out_specs = [tok(A_COLS), tok(3 * LANES), tok(2 * LANES), tok(2 * LANES)]
    for d in C_DILATIONS:
        for _ in range(3):
            if d == 1:
                out_shape.append(jax.ShapeDtypeStruct((n, t, LANES), BF16))
                out_specs.append(tok(LANES))
            else:
                out_shape.append(jax.ShapeDtypeStruct((n, d, t // d, LANES), BF16))
                out_specs.append(dil(d))
    rope_spec = pl.BlockSpec((tm, LANES), lambda b, i: (i, 0))
    return pl.pallas_call(
        _proj_kernel,
        grid=grid,
        in_specs=[tok(D_MODEL), const((1, D_MODEL)), const(w_in.shape), rope_spec, rope_spec,
                  const((1, LANES)), const((1, LANES)), const((1, LANES)), const((1, LANES))],
        out_specs=out_specs,
        out_shape=out_shape,
        scratch_shapes=[pltpu.VMEM((3 * len(C_DILATIONS), tm, LANES), F32)],
        compiler_params=pltpu.CompilerParams(dimension_semantics=("arbitrary", "arbitrary"),
                                             vmem_limit_bytes=VMEM_LIMIT),
        name="proj",
    )(x, g1, w_in, cos, sin, bqg, bkg, cqg, ckg)


def _hgrn_gates(u):
    c = GLA_CHUNK
    reverse, lb = u["reverse"], u["lb"]
    q, f_raw, v = u["load"]()
    half_t = 0.5 * jnp.tanh(0.5 * f_raw)
    f = lb + (1.0 - lb) * (0.5 + half_t)
    g = jnp.log2(jnp.maximum(f, GATE_FLOOR))
    kk = (1.0 - lb) * (0.5 - half_t)

    row = lax.broadcasted_iota(jnp.int32, (c, 2 * c), 0)
    col = lax.broadcasted_iota(jnp.int32, (c, 2 * c), 1) & (c - 1)
    tri2 = ((col >= row) if reverse else (col <= row)).astype(BF16)
    g1 = g.astype(BF16)
    g2 = (g - g1.astype(F32)).astype(BF16)
    b = _dot(tri2, jnp.concatenate([g1, g2], axis=0))
    u.update(q=q, v=v, kk=kk, b=b)


def _hgrn_state(u):
    c = GLA_CHUNK
    reverse, state_ref = u["reverse"], u["state_ref"]
    q, v, kk, b = u["q"], u["v"], u["kk"], u["b"]
    total = b[0:1] if reverse else b[c - 1:c]
    mid = b[c // 2:c // 2 + 1] if reverse else b[c // 2 - 1:c // 2]
    v16 = v.astype(BF16)

    qs = (q * jnp.exp2(b)).astype(BF16)
    ks = (kk * jnp.exp2(total - b)).astype(BF16)
    decay = jnp.exp2(total)
    srow = lax.broadcasted_iota(jnp.int32, (LANES, LANES), 0) >> 6
    scol = lax.broadcasted_iota(jnp.int32, (LANES, LANES), 1) >> 6
    same_head = srow == scol
    o = []
    for j in range(A_WIDTH // LANES):
        cols = slice(j * LANES, (j + 1) * LANES)
        state = state_ref[j]
        o.append(_dot_nt(qs[:, cols], state.astype(BF16)))
        upd = _dot_tn(v16[:, cols], ks[:, cols])
        state_ref[j] = state * decay[:, cols] + jnp.where(same_head, upd, 0.0)

    offset = jnp.maximum(jnp.abs(b[0:1] - mid), jnp.abs(b[c - 1:c] - mid))
    u.update(o=o, v16=v16, offset=offset)

    if u["exact_refs"] is None:
        qp = q * jnp.exp2(b - mid)
        kp = (kk * jnp.exp2(mid - b)).astype(BF16)
        lane_half = lax.broadcasted_iota(jnp.int32, (1, LANES), 1) >> 6
        t_idx = lax.broadcasted_iota(jnp.int32, (2 * c, c), 0) & (c - 1)
        s_idx = lax.broadcasted_iota(jnp.int32, (2 * c, c), 1)
        causal = (s_idx >= t_idx) if reverse else (s_idx <= t_idx)
        u["a16"] = []
        for j in range(A_WIDTH // LANES):
            cols = slice(j * LANES, (j + 1) * LANES)
            q_stack = jnp.concatenate([jnp.where(lane_half == h, qp[:, cols], 0.0) for h in range(2)],
                                      axis=0).astype(BF16)
            a = _dot_nt(q_stack, kp[:, cols])
            u["a16"].append(jnp.where(causal, a, 0.0).astype(BF16))


def _hgrn_output(u):
    c = GLA_CHUNK
    reverse = u["reverse"]
    if u["exact_refs"] is None:
        lane_half = lax.broadcasted_iota(jnp.int32, (1, LANES), 1) >> 6
        o = []
        for j in range(A_WIDTH // LANES):
            v_blk = u["v16"][:, j * LANES:(j + 1) * LANES]
            acc = u["o"][j]
            for h in range(2):
                acc = acc + _dot(u["a16"][j][h * c:(h + 1) * c], jnp.where(lane_half == h, v_blk, 0))
            o.append(acc)
        o = jnp.concatenate(o, axis=1)
    else:
        q, b = u["q"], u["b"]
        tb_ref, tk_ref, tv_ref = u["exact_refs"]
        tb_ref[...] = b
        tk_ref[...] = u["kk"]
        tv_ref[...] = u["v"]
        bd = _head_blockdiag(A_WIDTH)
        t_col = lax.broadcasted_iota(jnp.int32, (c, 1), 0)

        def body(si, acc):
            b_s = tb_ref[pl.ds(si, 1), :]
            dec = jnp.exp2(jnp.minimum(b - b_s, 0.0))
            valid = (t_col <= si) if reverse else (t_col >= si)
            p = jnp.where(valid, q * dec * tk_ref[pl.ds(si, 1), :], 0.0)
            return acc + _dot(p.astype(BF16), bd) * tv_ref[pl.ds(si, 1), :]

        o = lax.fori_loop(0, c, body, jnp.concatenate(u["o"], axis=1))
    u["store"](o)


def _hgrn_kernel(lb_ref, qf_ref, ff_ref, vf_ref, qb_ref, fb_ref, vb_ref, of_ref, ob_ref,
                 sf_ref, sb_ref, sf0_ref, sb0_ref, tb_ref, tk_ref, tv_ref, *, layer, n_chunks):
    @pl.when(pl.program_id(1) == 0)
    def _():
        sf_ref[...] = jnp.zeros_like(sf_ref)
        sb_ref[...] = jnp.zeros_like(sb_ref)

    sf0_ref[...] = sf_ref[...]
    sb0_ref[...] = sb_ref[...]

    raw = lb_ref[...]
    depth = raw.shape[0]
    mx = raw[0:1]
    for i in range(1, depth):
        mx = jnp.maximum(mx, raw[i:i + 1])
    e = jnp.exp(raw - mx)
    den = e[0:1]
    for i in range(1, depth):
        den = den + e[i:i + 1]
    sm = e / den
    cs = sm[0:1]
    for i in range(1, layer + 1):
        cs = cs + sm[i:i + 1]
    lb = cs - sm[0:1]
    lb_f = lb[:, 0:A_WIDTH]
    lb_b = lb[:, A_WIDTH:2 * A_WIDTH]

    def chunk_rows(ci):
        start = ci * GLA_CHUNK
        return pl.ds(start if isinstance(ci, int) else pl.multiple_of(start, GLA_CHUNK), GLA_CHUNK)

    def chunk_pair(ci, exact_refs):
        def record(q_ref, f_ref, v_ref, o_ref, rows, lb_d, state_ref, reverse):
            def store(o):
                o_ref[0, rows, :] = o
            return {"load": lambda: (q_ref[0, rows, :], f_ref[0, rows, :], v_ref[0, rows, :]), "store": store,
                    "lb": lb_d, "state_ref": state_ref, "reverse": reverse, "exact_refs": exact_refs}
        return [record(qf_ref, ff_ref, vf_ref, of_ref, chunk_rows(ci), lb_f, sf_ref, False),
                record(qb_ref, fb_ref, vb_ref, ob_ref, chunk_rows(n_chunks - 1 - ci), lb_b, sb_ref, True)]

    units = [u for ci in range(n_chunks) for u in chunk_pair(ci, None)]
    stages = (_hgrn_gates, _hgrn_state, _hgrn_output)
    skew = HGRN_STAGE_SKEW
    for i in range(len(units) + skew * (len(stages) - 1)):
        for k, stage in enumerate(stages):
            if 0 <= i - k * skew < len(units):
                stage(units[i - k * skew])
    worst = units[0]["offset"]
    for u in units[1:]:
        worst = jnp.maximum(worst, u["offset"])

    @pl.when(jnp.max(worst) > GLA_SAFE_RANGE)
    def _():
        sf_ref[...] = sf0_ref[...]
        sb_ref[...] = sb0_ref[...]

        def step(ci, carry):
            for u in chunk_pair(ci, (tb_ref, tk_ref, tv_ref)):
                for stage in stages:
                    stage(u)
            return carry

        lax.fori_loop(0, n_chunks, step, 0)


def _hgrn_call(a_raw, lb_raw2, layer):
    n, t, _ = a_raw.shape
    tt = HGRN_TILE
    nt = t // tt
    fwd = lambda cb: pl.BlockSpec((1, tt, A_WIDTH), lambda b, i: (b, i, cb))
    bwd = lambda cb: pl.BlockSpec((1, tt, A_WIDTH), lambda b, i: (b, nt - 1 - i, cb))
    return pl.pallas_call(
        functools.partial(_hgrn_kernel, layer=layer, n_chunks=tt // GLA_CHUNK),
        grid=(n, nt),
        in_specs=[pl.BlockSpec(lb_raw2.shape, lambda b, i: (0, 0)),
                  fwd(0), fwd(1), fwd(3), bwd(0), bwd(2), bwd(3)],
        out_specs=[fwd(0), bwd(0)],
        out_shape=[jax.ShapeDtypeStruct((n, t, A_WIDTH), F32)] * 2,
        scratch_shapes=[pltpu.VMEM((A_WIDTH // LANES, LANES, LANES), F32)] * 4
        + [pltpu.VMEM((GLA_CHUNK, A_WIDTH), F32)] * 3,
        compiler_params=pltpu.CompilerParams(dimension_semantics=("arbitrary", "arbitrary"),
                                             vmem_limit_bytes=VMEM_LIMIT),
        name="hgrn",
    )(lb_raw2, a_raw, a_raw, a_raw, a_raw, a_raw, a_raw)


def _attn_kernel(*refs, w, seq_len, heads, use_sink, want_lse):
    q_ref, kp_ref, kc_ref, kn_ref, vp_ref, vc_ref, vn_ref = refs[:7]
    rest = list(refs[7:])
    sink_ref = rest.pop(0) if use_sink else None
    o_ref = rest.pop(0)
    lse_ref = rest.pop(0) if want_lse else None

    n_seq, tr = q_ref.shape[0], q_ref.shape[1]
    sb_rows, sb_keys = ATTN_SUB_ROWS, ATTN_SUB_ROWS + 2 * w
    n_sb = tr // sb_rows
    t0 = pl.program_id(1) * tr
    lane_half = lax.broadcasted_iota(jnp.int32, (1, LANES), 1) >> 6
    row = lax.broadcasted_iota(jnp.int32, (sb_rows, sb_keys), 0)
    col = lax.broadcasted_iota(jnp.int32, (sb_rows, sb_keys), 1)
    band = (col >= row) & (col - row <= 2 * w)

    kv_cache = {}

    def kv_variants(sq):
        if sq not in kv_cache:
            k_ext = jnp.concatenate([kp_ref[sq], kc_ref[sq], kn_ref[sq]], axis=0)
            v_ext = jnp.concatenate([vp_ref[sq], vc_ref[sq], vn_ref[sq]], axis=0)
            kv = {}
            for blk_heads in heads:
                for half, kv_blk in blk_heads:
                    if (half, kv_blk) not in kv:
                        sel = lane_half == half
                        kv[(half, kv_blk)] = (jnp.where(sel, k_ext[:, kv_blk * LANES:(kv_blk + 1) * LANES], 0),
                                              jnp.where(sel, v_ext[:, kv_blk * LANES:(kv_blk + 1) * LANES], 0))
            kv_cache[sq] = kv
        return kv_cache[sq]

    mask_cache = {}

    def tile_mask(sb):
        if sb not in mask_cache:
            mask = band
            if sb == 0 or sb == n_sb - 1:
                key_pos = col + (t0 - w + sb * sb_rows)
                mask = band & (key_pos >= 0) & (key_pos < seq_len)
            mask_cache[sb] = mask
        return mask_cache[sb]

    def stage_scores(u):
        sq, sb, j = u["id"]
        r0 = sb * sb_rows
        q = q_ref[sq, r0:r0 + sb_rows, j * LANES:(j + 1) * LANES]
        u["s"], u["m"], u["sink"] = [], [], []
        for e, (half, kv_blk) in enumerate(heads[j]):
            k_h = kv_variants(sq)[(half, kv_blk)][0]
            s = jnp.where(tile_mask(sb), _dot_nt(q, k_h[r0:r0 + sb_keys]), MASK_VALUE)
            m = jnp.max(s, axis=-1, keepdims=True)
            sink = None
            if use_sink:
                sink = sink_ref[2 * j + e] * LOG2_E
                m = jnp.maximum(m, sink)
            u["s"].append(s)
            u["m"].append(m)
            u["sink"].append(sink)

    def stage_probs(u):
        u["p"], u["den"] = [], []
        for s, m, sink in zip(u["s"], u["m"], u["sink"]):
            p = jnp.exp2(s - m)
            den = jnp.sum(p, axis=-1, keepdims=True)
            if use_sink:
                den = den + jnp.exp2(sink - m)
            u["p"].append(p.astype(BF16))
            u["den"].append(den)
        u["s"] = None

    def stage_out(u):
        sq, sb, j = u["id"]
        r0 = sb * sb_rows
        out = None
        lse_out = None
        for e, (half, kv_blk) in enumerate(heads[j]):
            v_h = kv_variants(sq)[(half, kv_blk)][1]
            o = _dot(u["p"][e], v_h[r0:r0 + sb_keys]) * (1.0 / u["den"][e])
            out = o if out is None else out + o
            if want_lse:
                lse = u["m"][e] + jnp.log2(u["den"][e])
                lse_out = lse if lse_out is None else jnp.where(lane_half == half, lse, lse_out)
        o_ref[sq, r0:r0 + sb_rows, j * LANES:(j + 1) * LANES] = out.astype(o_ref.dtype)
        if want_lse:
            lse_ref[sq, r0:r0 + sb_rows, :] = jnp.broadcast_to(lse_out, (sb_rows, LANES))

    units = [{"id": (sq, sb, j)} for sq in range(n_seq) for sb in range(n_sb) for j in range(len(heads))]
    skew = ATTN_STAGE_SKEW
    for i in range(len(units) + 2 * skew):
        if i < len(units):
            stage_scores(units[i])
        if 0 <= i - skew < len(units):
            stage_probs(units[i - skew])
        if 0 <= i - 2 * skew < len(units):
            stage_out(units[i - 2 * skew])


def _attn_call(q, k, v, sink, *, w, heads, want_lse, rows_per_step, name):
    n, seq_len, qw = q.shape
    kw = k.shape[-1]
    tr = min(rows_per_step, seq_len)
    nb = min(rows_per_step // tr, n)
    assert seq_len % tr == 0 and n % nb == 0 and tr % ATTN_SUB_ROWS == 0 and tr % w == 0
    per = tr // w
    last = seq_len // w - 1
    cur = lambda width: pl.BlockSpec((nb, tr, width), lambda b, i: (b, i, 0))
    prev = pl.BlockSpec((nb, w, kw), lambda b, i: (b, jnp.maximum(i * per - 1, 0), 0))
    nxt = pl.BlockSpec((nb, w, kw), lambda b, i: (b, jnp.minimum((i + 1) * per, last), 0))
    in_specs = [cur(qw), prev, cur(kw), nxt, prev, cur(kw), nxt]
    args = [q, k, k, k, v, v, v]
    if sink is not None:
        in_specs.append(pl.BlockSpec(memory_space=pltpu.SMEM))
        args.append(sink)
    out_shape = [jax.ShapeDtypeStruct((n, seq_len, qw), BF16)]
    out_specs = [cur(qw)]
    if want_lse:
        out_shape.append(jax.ShapeDtypeStruct((n, seq_len, LANES), F32))
        out_specs.append(cur(LANES))
    return pl.pallas_call(
        functools.partial(_attn_kernel, w=w, seq_len=seq_len, heads=heads, use_sink=sink is not None,
                          want_lse=want_lse),
        grid=(n // nb, seq_len // tr),
        in_specs=in_specs,
        out_specs=out_specs,
        out_shape=out_shape,
        compiler_params=pltpu.CompilerParams(dimension_semantics=("arbitrary", "arbitrary"),
                                             vmem_limit_bytes=VMEM_LIMIT),
        name=name,
    )(*args)


_B_HEADS = tuple(tuple((h % 2, 0 if (h // 3) == (h % 2) else 1) for h in (2 * j, 2 * j + 1)) for j in range(3))
_C_HEADS = (((0, 0), (1, 0)),)


def _outffn_kernel(x_ref, of_ref, ob_ref, gate_ref, ag_ref, yb_ref,
                   o1_ref, l1_ref, o4_ref, l4_ref, o16_ref, l16_ref,
                   wout_ref, n2_ref, wg_ref, wu_ref, wd_ref,
                   y_ref, s4o_ref, s4l_ref, s16o_ref, s16l_ref):
    tm = x_ref.shape[1]
    bd = _head_mean_matrix(LANES)
    ya = []
    for j in range(A_WIDTH // LANES):
        cols = slice(j * LANES, (j + 1) * LANES)
        o = of_ref[0, :, cols] + ob_ref[0, :, cols]
        ya.append(_head_rmsnorm(o, ag_ref[...], bd) * _silu(gate_ref[0, :, cols]))
    ya = jnp.concatenate(ya, axis=1).astype(BF16)

    def interleave(o_ref, l_ref, so_ref, sl_ref, d):
        for r in range(d):
            so_ref[pl.ds(r, tm // d, stride=d), :] = o_ref[0, r].astype(F32)
            sl_ref[pl.ds(r, tm // d, stride=d), :] = l_ref[0, r]
        return so_ref[...], sl_ref[...]

    o1, l1 = o1_ref[0].astype(F32), l1_ref[0]
    o4, l4 = interleave(o4_ref, l4_ref, s4o_ref, s4l_ref, 4)
    o16, l16 = interleave(o16_ref, l16_ref, s16o_ref, s16l_ref, 16)
    m = jnp.maximum(jnp.maximum(l1, l4), l16)
    e1, e4, e16 = jnp.exp2(l1 - m), jnp.exp2(l4 - m), jnp.exp2(l16 - m)
    yc = ((e1 * o1 + e4 * o4 + e16 * o16) / (e1 + e4 + e16)).astype(BF16)

    h = x_ref[0] + _dot(jnp.concatenate([ya, yb_ref[0], yc], axis=1), wout_ref[...])
    hn = (h * lax.rsqrt(jnp.mean(h * h, axis=-1, keepdims=True) + NORM_EPS) * n2_ref[...]).astype(BF16)

    y = h
    c0 = 0
    for width in FF_GROUPS:
        cols = slice(c0, c0 + width)
        act = (_silu(_dot(hn, wg_ref[:, cols])) * _dot(hn, wu_ref[:, cols])).astype(BF16)
        y = y + _dot(act, wd_ref[cols, :])
        c0 += width
    y_ref[0] = y


def _outffn_call(x, o_f, o_b, a_raw, ag, yb, c_out, w_out, n2, wg, wu, wd):
    n, t, _ = x.shape
    tm = OUT_TOKEN_TILE
    tok = lambda width, cb=0: pl.BlockSpec((1, tm, width), lambda b, i: (b, i, cb))
    dil = lambda d: pl.BlockSpec((1, d, tm // d, LANES), lambda b, i: (b, 0, i, 0))
    const = lambda shape: pl.BlockSpec(shape, lambda b, i: (0,) * len(shape), pipeline_mode=pl.Buffered(1))
    (o1, l1), (o4, l4), (o16, l16) = c_out
    return pl.pallas_call(
        _outffn_kernel,
        grid=(n, t // tm),
        in_specs=[tok(D_MODEL), tok(A_WIDTH), tok(A_WIDTH), tok(A_WIDTH, 4), const((1, LANES)), tok(3 * LANES),
                  tok(LANES), tok(LANES), dil(4), dil(4), dil(16), dil(16),
                  const(w_out.shape), const((1, D_MODEL)), const(wg.shape), const(wu.shape), const(wd.shape)],
        out_specs=tok(D_MODEL),
        out_shape=jax.ShapeDtypeStruct((n, t, D_MODEL), F32),
        scratch_shapes=[pltpu.VMEM((tm, LANES), F32)] * 4,
        compiler_params=pltpu.CompilerParams(dimension_semantics=("arbitrary", "arbitrary"),
                                             vmem_limit_bytes=VMEM_LIMIT),
        name="outffn",
    )(x, o_f, o_b, a_raw, ag, yb, o1, l1, o4, l4, o16, l16, w_out, n2, wg, wu, wd)


def _layer(x, p, layer, cos, sin):
    n, t, _ = x.shape
    outs = _proj_call(x, p["n1"], p["w_in"], cos, sin, p["bqg"], p["bkg"], p["cqg"], p["ckg"])
    a_raw, bq, bk, bv = outs[:4]
    o_f, o_b = _hgrn_call(a_raw, p["lb_raw"], layer)
    (yb,) = _attn_call(bq, bk, bv, p["sink"], w=B_HALF_WINDOW, heads=_B_HEADS, want_lse=False,
                       rows_per_step=ATTN_ROWS_WINDOW, name="attn_window")
    c_out = []
    for i, d in enumerate(C_DILATIONS):
        q, k, v = (a.reshape(n * d, t // d, LANES) for a in outs[4 + 3 * i:7 + 3 * i])
        o, lse = _attn_call(q, k, v, None, w=C_HALF_WINDOW, heads=_C_HEADS, want_lse=True,
                            rows_per_step=ATTN_ROWS_DILATED, name=f"attn_dil{d}")
        shape = (n, t, LANES) if d == 1 else (n, d, t // d, LANES)
        c_out.append((o.reshape(shape), lse.reshape(shape)))
    return _outffn_call(x, o_f, o_b, a_raw, p["ag"], yb, c_out, p["w_out"], p["n2"], p["wg"], p["wu"], p["wd"])


def _layer_params(l, norm1_g, w_in, lb_raw, a_norm_g, b_qn_g, b_kn_g, b_sink, c_qn_g, c_kn_g, w_out, norm2_g,
                  w_gate, w_up, w_down):
    pair = lambda g: jnp.tile(g[l].astype(F32), LANES // HEAD_DIM)[None, :]
    return {
        "n1": norm1_g[l].astype(F32)[None, :],
        "w_in": w_in[l].astype(BF16),
        "lb_raw": lb_raw.astype(F32).reshape(lb_raw.shape[0], 2 * A_WIDTH),
        "ag": pair(a_norm_g), "bqg": pair(b_qn_g), "bkg": pair(b_kn_g), "cqg": pair(c_qn_g), "ckg": pair(c_kn_g),
        "sink": b_sink[l].astype(F32),
        "w_out": w_out[l].astype(BF16),
        "n2": norm2_g[l].astype(F32)[None, :],
        "wg": w_gate[l].astype(BF16),
        "wu": w_up[l].astype(BF16),
        "wd": w_down[l].astype(BF16),
    }


def kernel(x_prompt, x_sample, norm1_g, w_in, lb_raw, a_norm_g, b_qn_g, b_kn_g, b_sink, c_qn_g, c_kn_g, w_out,
           norm2_g, w_gate, w_up, w_down):
    depth = w_in.shape[0]
    params = [_layer_params(l, norm1_g, w_in, lb_raw, a_norm_g, b_qn_g, b_kn_g, b_sink, c_qn_g, c_kn_g, w_out,
                            norm2_g, w_gate, w_up, w_down) for l in range(depth)]

    def trunk(x):
        cos, sin = _rope_tables(x.shape[1])
        for l in range(depth):
            x = _layer(x, params[l], l, cos, sin)
        return x

    return (trunk(x_prompt), trunk(x_sample))
```

```python
import functools

import jax
import jax.numpy as jnp
import numpy as np
from jax import lax
from jax.experimental import pallas as pl
from jax.experimental.pallas import tpu as pltpu

F32 = jnp.float32
BF16 = jnp.bfloat16

D_MODEL = 1024
HEAD_DIM = 64
LANES = 128
A_WIDTH = 256
A_COLS = 5 * A_WIDTH
BQ_COL, BK_COL, BV_COL = 1280, 1664, 1792
CQ_COL, CK_COL, CV_COL = 1920, 2304, 2688
B_HALF_WINDOW = 128
C_HALF_WINDOW = 64
C_DILATIONS = (1, 4, 16)
D_FF = 2816
FF_GROUPS = (1024, 1024, 768)
ROPE_THETA = 10000.0
NORM_EPS = 1e-6
MASK_VALUE = -1e30
GATE_FLOOR = 1e-30
GLA_CHUNK = 64
GLA_SAFE_RANGE = 110.0

TOKEN_TILE = 1024
OUT_TOKEN_TILE = 512
HGRN_TILE = 1024
ATTN_ROWS_WINDOW = 2048
ATTN_ROWS_DILATED = 4096
HGRN_STAGE_SKEW = 2
ATTN_SUB_ROWS_WINDOW = 128
ATTN_SUB_ROWS_DILATED = 128
ATTN_STAGE_SKEW = 1
LOG2_E = 1.4426950408889634
VMEM_LIMIT = 56 * 1024 * 1024


def _dot(a, b):
    return jnp.dot(a, b, preferred_element_type=F32)


def _dot_nt(a, b):
    return lax.dot_general(a, b, (((1,), (1,)), ((), ())), preferred_element_type=F32)


def _dot_tn(a, b):
    return lax.dot_general(a, b, (((0,), (0,)), ((), ())), preferred_element_type=F32)


def _head_blockdiag(n):
    r = lax.broadcasted_iota(jnp.int32, (n, n), 0) >> 6
    c = lax.broadcasted_iota(jnp.int32, (n, n), 1) >> 6
    return (r == c).astype(BF16)


def _head_mean_matrix(n):
    bd = _head_blockdiag(n) * (1.0 / HEAD_DIM)
    return jnp.concatenate([bd, bd], axis=0).astype(BF16)


def _head_mean(x, mean2):
    hi = x.astype(BF16)
    lo = (x - hi.astype(F32)).astype(BF16)
    return _dot(jnp.concatenate([hi, lo], axis=1), mean2)


def _head_rmsnorm(y, g, mean2):
    return y * lax.rsqrt(_head_mean(y * y, mean2) + NORM_EPS) * g


def _silu(x):
    half = 0.5 * x
    return half * jnp.tanh(half) + half


def _proj_kernel(x_ref, g1_ref, w_ref, cos_ref, sin_ref, bqg_ref, bkg_ref, cqg_ref, ckg_ref,
                 a_ref, bq_ref, bk_ref, bv_ref,
                 cq1_ref, ck1_ref, cv1_ref, cq4_ref, ck4_ref, cv4_ref, cq16_ref, ck16_ref, cv16_ref,
                 scr_ref):
    tm = x_ref.shape[1]
    x = x_ref[0]
    xn = (x * lax.rsqrt(jnp.mean(x * x, axis=-1, keepdims=True) + NORM_EPS) * g1_ref[...]).astype(BF16)
    y_bc = _dot(xn, w_ref[:, A_COLS:])

    bd = _head_mean_matrix(LANES)
    cos = cos_ref[...]
    sin = sin_ref[...]
    lane = lax.broadcasted_iota(jnp.int32, (1, LANES), 1)
    first_half = (lane & (HEAD_DIM - 1)) < (HEAD_DIM // 2)

    def proj(col):
        return y_bc[:, col - A_COLS:col - A_COLS + LANES]

    def norm_rope(col, gain):
        y = _head_rmsnorm(proj(col), gain, bd)
        partner = jnp.where(first_half, pltpu.roll(y, LANES - HEAD_DIM // 2, 1), pltpu.roll(y, HEAD_DIM // 2, 1))
        return y * cos + partner * sin

    scale = LOG2_E * HEAD_DIM ** -0.5
    bq_gain, bk_gain = bqg_ref[...] * scale, bkg_ref[...]
    cq_gain, ck_gain = cqg_ref[...] * scale, ckg_ref[...]
    for j in range(3):
        bq_ref[0, :, j * LANES:(j + 1) * LANES] = norm_rope(BQ_COL + j * LANES, bq_gain).astype(BF16)
    bk = norm_rope(BK_COL, bk_gain)
    bk_ref[0, :, 0:LANES] = bk.astype(BF16)
    bk_ref[0, :, LANES:2 * LANES] = pltpu.roll(bk, HEAD_DIM, 1).astype(BF16)
    bv = proj(BV_COL)
    bv_ref[0, :, 0:LANES] = bv.astype(BF16)
    bv_ref[0, :, LANES:2 * LANES] = pltpu.roll(bv, HEAD_DIM, 1).astype(BF16)

    def store_dilated(ref, y, d, slot):
        if d == 1:
            ref[0] = y.astype(BF16)
            return
        scr_ref[slot] = y
        for r in range(d):
            ref[0, r] = scr_ref[slot, pl.ds(r, tm // d, stride=d), :].astype(BF16)

    c_refs = ((cq1_ref, ck1_ref, cv1_ref), (cq4_ref, ck4_ref, cv4_ref), (cq16_ref, ck16_ref, cv16_ref))
    for p, d in enumerate(C_DILATIONS):
        q_ref, k_ref, v_ref = c_refs[p]
        store_dilated(q_ref, norm_rope(CQ_COL + p * LANES, cq_gain), d, 3 * p)
        store_dilated(k_ref, norm_rope(CK_COL + p * LANES, ck_gain), d, 3 * p + 1)
        store_dilated(v_ref, proj(CV_COL + p * LANES), d, 3 * p + 2)

    a_ref[0] = _dot(xn, w_ref[:, 0:A_COLS])


def _rope_tables(t_len):
    half = HEAD_DIM // 2
    inv = ROPE_THETA ** (-jnp.arange(half, dtype=F32) / half)
    ang = jnp.arange(t_len).astype(F32)[:, None] * inv[None, :]
    cos = jnp.tile(jnp.cos(ang), (1, 2 * LANES // HEAD_DIM))
    sin = jnp.sin(ang)
    sin = jnp.tile(jnp.concatenate([-sin, sin], axis=1), (1, LANES // HEAD_DIM))
    return cos, sin


def _proj_call(x, g1, w_in, cos, sin, bqg, bkg, cqg, ckg):
    n, t, _ = x.shape
    tm = TOKEN_TILE
    grid = (n, t // tm)
    tok = lambda width: pl.BlockSpec((1, tm, width), lambda b, i: (b, i, 0))
    const = lambda shape: pl.BlockSpec(shape, lambda b, i: (0,) * len(shape), pipeline_mode=pl.Buffered(1))
    dil = lambda d: pl.BlockSpec((1, d, tm // d, LANES), lambda b, i: (b, 0, i, 0))
    out_shape = [jax.ShapeDtypeStruct((n, t, A_COLS), F32),
                 jax.ShapeDtypeStruct((n, t, 3 * LANES), BF16),
                 jax.ShapeDtypeStruct((n, t, 2 * LANES), BF16),
                 jax.ShapeDtypeStruct((n, t, 2 * LANES), BF16)]
    out_specs = [tok(A_COLS), tok(3 * LANES), tok(2 * LANES), tok(2 * LANES)]
    for d in C_DILATIONS:
        for _ in range(3):
            if d == 1:
                out_shape.append(jax.ShapeDtypeStruct((n, t, LANES), BF16))
                out_specs.append(tok(LANES))
            else:
                out_shape.append(jax.ShapeDtypeStruct((n, d, t // d, LANES), BF16))
                out_specs.append(dil(d))
    rope_spec = pl.BlockSpec((tm, LANES), lambda b, i: (i, 0))
    return pl.pallas_call(
        _proj_kernel,
        grid=grid,
        in_specs=[tok(D_MODEL), const((1, D_MODEL)), const(w_in.shape), rope_spec, rope_spec,
                  const((1, LANES)), const((1, LANES)), const((1, LANES)), const((1, LANES))],
        out_specs=out_specs,
        out_shape=out_shape,
        scratch_shapes=[pltpu.VMEM((3 * len(C_DILATIONS), tm, LANES), F32)],
        compiler_params=pltpu.CompilerParams(dimension_semantics=("arbitrary", "arbitrary"),
                                             vmem_limit_bytes=VMEM_LIMIT),
        name="proj",
    )(x, g1, w_in, cos, sin, bqg, bkg, cqg, ckg)


def _hgrn_gates(u):
    c = GLA_CHUNK
    reverse, lb = u["reverse"], u["lb"]
    q, f_raw, v = u["load"]()
    half_t = 0.5 * jnp.tanh(0.5 * f_raw)
    f = lb + (1.0 - lb) * (0.5 + half_t)
    g = jnp.log2(jnp.maximum(f, GATE_FLOOR))
    kk = (1.0 - lb) * (0.5 - half_t)

    row = lax.broadcasted_iota(jnp.int32, (c, 2 * c), 0)
    col = lax.broadcasted_iota(jnp.int32, (c, 2 * c), 1) & (c - 1)
    tri2 = ((col >= row) if reverse else (col <= row)).astype(BF16)
    g1 = g.astype(BF16)
    g2 = (g - g1.astype(F32)).astype(BF16)
    b = _dot(tri2, jnp.concatenate([g1, g2], axis=0))
    u.update(q=q, v=v, kk=kk, b=b)


def _hgrn_state(u):
    c = GLA_CHUNK
    reverse, state_ref = u["reverse"], u["state_ref"]
    q, v, kk, b = u["q"], u["v"], u["kk"], u["b"]
    total = b[0:1] if reverse else b[c - 1:c]
    mid = b[c // 2:c // 2 + 1] if reverse else b[c // 2 - 1:c // 2]
    v16 = v.astype(BF16)

    qs = (q * jnp.exp2(b)).astype(BF16)
    ks = (kk * jnp.exp2(total - b)).astype(BF16)
    decay = jnp.exp2(total)
    srow = lax.broadcasted_iota(jnp.int32, (LANES, LANES), 0) >> 6
    scol = lax.broadcasted_iota(jnp.int32, (LANES, LANES), 1) >> 6
    same_head = srow == scol
    o = []
    for j in range(A_WIDTH // LANES):
        cols = slice(j * LANES, (j + 1) * LANES)
        state = state_ref[j]
        o.append(_dot_nt(qs[:, cols], state.astype(BF16)))
        upd = _dot_tn(v16[:, cols], ks[:, cols])
        state_ref[j] = state * decay[:, cols] + jnp.where(same_head, upd, 0.0)

    offset = jnp.maximum(jnp.abs(b[0:1] - mid), jnp.abs(b[c - 1:c] - mid))
    u.update(o=o, v16=v16, offset=offset)

    if u["exact_refs"] is None:
        qp = q * jnp.exp2(b - mid)
        kp = (kk * jnp.exp2(mid - b)).astype(BF16)
        lane_half = lax.broadcasted_iota(jnp.int32, (1, LANES), 1) >> 6
        t_idx = lax.broadcasted_iota(jnp.int32, (2 * c, c), 0) & (c - 1)
        s_idx = lax.broadcasted_iota(jnp.int32, (2 * c, c), 1)
        causal = (s_idx >= t_idx) if reverse else (s_idx <= t_idx)
        u["a16"] = []
        for j in range(A_WIDTH // LANES):
            cols = slice(j * LANES, (j + 1) * LANES)
            q_stack = jnp.concatenate([jnp.where(lane_half == h, qp[:, cols], 0.0) for h in range(2)],
                                      axis=0).astype(BF16)
            a = _dot_nt(q_stack, kp[:, cols])
            u["a16"].append(jnp.where(causal, a, 0.0).astype(BF16))


def _hgrn_output(u):
    c = GLA_CHUNK
    reverse = u["reverse"]
    if u["exact_refs"] is None:
        lane_half = lax.broadcasted_iota(jnp.int32, (1, LANES), 1) >> 6
        o = []
        for j in range(A_WIDTH // LANES):
            v_blk = u["v16"][:, j * LANES:(j + 1) * LANES]
            acc = u["o"][j]
            for h in range(2):
                acc = acc + _dot(u["a16"][j][h * c:(h + 1) * c], jnp.where(lane_half == h, v_blk, 0))
            o.append(acc)
        o = jnp.concatenate(o, axis=1)
    else:
        q, b = u["q"], u["b"]
        tb_ref, tk_ref, tv_ref = u["exact_refs"]
        tb_ref[...] = b
        tk_ref[...] = u["kk"]
        tv_ref[...] = u["v"]
        bd = _head_blockdiag(A_WIDTH)
        t_col = lax.broadcasted_iota(jnp.int32, (c, 1), 0)

        def body(si, acc):
            b_s = tb_ref[pl.ds(si, 1), :]
            dec = jnp.exp2(jnp.minimum(b - b_s, 0.0))
            valid = (t_col <= si) if reverse else (t_col >= si)
            p = jnp.where(valid, q * dec * tk_ref[pl.ds(si, 1), :], 0.0)
            return acc + _dot(p.astype(BF16), bd) * tv_ref[pl.ds(si, 1), :]

        o = lax.fori_loop(0, c, body, jnp.concatenate(u["o"], axis=1))
    u["store"](o)


def _hgrn_kernel(lb_ref, qf_ref, ff_ref, vf_ref, qb_ref, fb_ref, vb_ref, of_ref, ob_ref,
                 sf_ref, sb_ref, sf0_ref, sb0_ref, tb_ref, tk_ref, tv_ref, *, layer, n_chunks):
    @pl.when(pl.program_id(1) == 0)
    def _():
        sf_ref[...] = jnp.zeros_like(sf_ref)
        sb_ref[...] = jnp.zeros_like(sb_ref)

    sf0_ref[...] = sf_ref[...]
    sb0_ref[...] = sb_ref[...]

    raw = lb_ref[...]
    depth = raw.shape[0]
    mx = raw[0:1]
    for i in range(1, depth):
        mx = jnp.maximum(mx, raw[i:i + 1])
    e = jnp.exp(raw - mx)
    den = e[0:1]
    for i in range(1, depth):
        den = den + e[i:i + 1]
    sm = e / den
    cs = sm[0:1]
    for i in range(1, layer + 1):
        cs = cs + sm[i:i + 1]
    lb = cs - sm[0:1]
    lb_f = lb[:, 0:A_WIDTH]
    lb_b = lb[:, A_WIDTH:2 * A_WIDTH]

    def chunk_rows(ci):
        start = ci * GLA_CHUNK
        return pl.ds(start if isinstance(ci, int) else pl.multiple_of(start, GLA_CHUNK), GLA_CHUNK)

    def chunk_pair(ci, exact_refs):
        def record(q_ref, f_ref, v_ref, o_ref, rows, lb_d, state_ref, reverse):
            def store(o):
                o_ref[0, rows, :] = o
            return {"load": lambda: (q_ref[0, rows, :], f_ref[0, rows, :], v_ref[0, rows, :]), "store": store,
                    "lb": lb_d, "state_ref": state_ref, "reverse": reverse, "exact_refs": exact_refs}
        return [record(qf_ref, ff_ref, vf_ref, of_ref, chunk_rows(ci), lb_f, sf_ref, False),
                record(qb_ref, fb_ref, vb_ref, ob_ref, chunk_rows(n_chunks - 1 - ci), lb_b, sb_ref, True)]

    units = [u for ci in range(n_chunks) for u in chunk_pair(ci, None)]
    stages = (_hgrn_gates, _hgrn_state, _hgrn_output)
    skew = HGRN_STAGE_SKEW
    for i in range(len(units) + skew * (len(stages) - 1)):
        for k, stage in enumerate(stages):
            if 0 <= i - k * skew < len(units):
                stage(units[i - k * skew])
    worst = units[0]["offset"]
    for u in units[1:]:
        worst = jnp.maximum(worst, u["offset"])

    @pl.when(jnp.max(worst) > GLA_SAFE_RANGE)
    def _():
        sf_ref[...] = sf0_ref[...]
        sb_ref[...] = sb0_ref[...]

        def step(ci, carry):
            for u in chunk_pair(ci, (tb_ref, tk_ref, tv_ref)):
                for stage in stages:
                    stage(u)
            return carry

        lax.fori_loop(0, n_chunks, step, 0)


def _hgrn_call(a_raw, lb_raw2, layer):
    n, t, _ = a_raw.shape
    tt = HGRN_TILE
    nt = t // tt
    fwd = lambda cb: pl.BlockSpec((1, tt, A_WIDTH), lambda b, i: (b, i, cb))
    bwd = lambda cb: pl.BlockSpec((1, tt, A_WIDTH), lambda b, i: (b, nt - 1 - i, cb))
    return pl.pallas_call(
        functools.partial(_hgrn_kernel, layer=layer, n_chunks=tt // GLA_CHUNK),
        grid=(n, nt),
        in_specs=[pl.BlockSpec(lb_raw2.shape, lambda b, i: (0, 0)),
                  fwd(0), fwd(1), fwd(3), bwd(0), bwd(2), bwd(3)],
        out_specs=[fwd(0), bwd(0)],
        out_shape=[jax.ShapeDtypeStruct((n, t, A_WIDTH), F32)] * 2,
        scratch_shapes=[pltpu.VMEM((A_WIDTH // LANES, LANES, LANES), F32)] * 4
        + [pltpu.VMEM((GLA_CHUNK, A_WIDTH), F32)] * 3,
        compiler_params=pltpu.CompilerParams(dimension_semantics=("arbitrary", "arbitrary"),
                                             vmem_limit_bytes=VMEM_LIMIT),
        name="hgrn",
    )(lb_raw2, a_raw, a_raw, a_raw, a_raw, a_raw, a_raw)


def _attn_kernel(*refs, w, seq_len, heads, use_sink, want_lse, sub_rows):
    q_ref, kp_ref, kc_ref, kn_ref, vp_ref, vc_ref, vn_ref = refs[:7]
    rest = list(refs[7:])
    sink_ref = rest.pop(0) if use_sink else None
    o_ref = rest.pop(0)
    lse_ref = rest.pop(0) if want_lse else None

    n_seq, tr = q_ref.shape[0], q_ref.shape[1]
    sb_rows, sb_keys = sub_rows, sub_rows + 2 * w
    n_sb = tr // sb_rows
    t0 = pl.program_id(1) * tr
    lane_half = lax.broadcasted_iota(jnp.int32, (1, LANES), 1) >> 6
    row = lax.broadcasted_iota(jnp.int32, (sb_rows, sb_keys), 0)
    col = lax.broadcasted_iota(jnp.int32, (sb_rows, sb_keys), 1)
    band = (col >= row) & (col - row <= 2 * w)

    kv_cache = {}

    def kv_variants(sq):
        if sq not in kv_cache:
            k_ext = jnp.concatenate([kp_ref[sq], kc_ref[sq], kn_ref[sq]], axis=0)
            v_ext = jnp.concatenate([vp_ref[sq], vc_ref[sq], vn_ref[sq]], axis=0)
            kv = {}
            for blk_heads in heads:
                for half, kv_blk in blk_heads:
                    if (half, kv_blk) not in kv:
                        sel = lane_half == half
                        kv[(half, kv_blk)] = (jnp.where(sel, k_ext[:, kv_blk * LANES:(kv_blk + 1) * LANES], 0),
                                              jnp.where(sel, v_ext[:, kv_blk * LANES:(kv_blk + 1) * LANES], 0))
            kv_cache[sq] = kv
        return kv_cache[sq]

    mask_cache = {}

    def tile_mask(sb):
        if sb not in mask_cache:
            mask = band
            if sb == 0 or sb == n_sb - 1:
                key_pos = col + (t0 - w + sb * sb_rows)
                mask = band & (key_pos >= 0) & (key_pos < seq_len)
            mask_cache[sb] = mask
        return mask_cache[sb]

    def stage_scores(u):
        sq, sb, j = u["id"]
        r0 = sb * sb_rows
        q = q_ref[sq, r0:r0 + sb_rows, j * LANES:(j + 1) * LANES]
        u["s"], u["m"], u["sink"] = [], [], []
        for e, (half, kv_blk) in enumerate(heads[j]):
            k_h = kv_variants(sq)[(half, kv_blk)][0]
            s = jnp.where(tile_mask(sb), _dot_nt(q, k_h[r0:r0 + sb_keys]), MASK_VALUE)
            m = jnp.max(s, axis=-1, keepdims=True)
            sink = None
            if use_sink:
                sink = sink_ref[2 * j + e] * LOG2_E
                m = jnp.maximum(m, sink)
            u["s"].append(s)
            u["m"].append(m)
            u["sink"].append(sink)

    def stage_probs(u):
        u["p"], u["den"] = [], []
        for s, m, sink in zip(u["s"], u["m"], u["sink"]):
            p = jnp.exp2(s - m)
            den = jnp.sum(p, axis=-1, keepdims=True)
            if use_sink:
                den = den + jnp.exp2(sink - m)
            u["p"].append(p.astype(BF16))
            u["den"].append(den)
        u["s"] = None

    def stage_out(u):
        sq, sb, j = u["id"]
        r0 = sb * sb_rows
        out = None
        lse_out = None
        for e, (half, kv_blk) in enumerate(heads[j]):
            v_h = kv_variants(sq)[(half, kv_blk)][1]
            o = _dot(u["p"][e], v_h[r0:r0 + sb_keys]) * (1.0 / u["den"][e])
            out = o if out is None else out + o
            if want_lse:
                lse = u["m"][e] + jnp.log2(u["den"][e])
                lse_out = lse if lse_out is None else jnp.where(lane_half == half, lse, lse_out)
        o_ref[sq, r0:r0 + sb_rows, j * LANES:(j + 1) * LANES] = out.astype(o_ref.dtype)
        if want_lse:
            lse_ref[sq, r0:r0 + sb_rows, :] = jnp.broadcast_to(lse_out, (sb_rows, LANES))

    units = [{"id": (sq, sb, j)} for sq in range(n_seq) for sb in range(n_sb) for j in range(len(heads))]
    skew = ATTN_STAGE_SKEW
    for i in range(len(units) + 2 * skew):
        if i < len(units):
            stage_scores(units[i])
        if 0 <= i - skew < len(units):
            stage_probs(units[i - skew])
        if 0 <= i - 2 * skew < len(units):
            stage_out(units[i - 2 * skew])


def _attn_call(q, k, v, sink, *, w, heads, want_lse, rows_per_step, sub_rows, name):
    n, seq_len, qw = q.shape
    kw = k.shape[-1]
    tr = min(rows_per_step, seq_len)
    nb = min(rows_per_step // tr, n)
    sub_rows = min(sub_rows, tr)
    assert seq_len % tr == 0 and n % nb == 0 and tr % sub_rows == 0 and tr % w == 0
    per = tr // w
    last = seq_len // w - 1
    cur = lambda width: pl.BlockSpec((nb, tr, width), lambda b, i: (b, i, 0))
    prev = pl.BlockSpec((nb, w, kw), lambda b, i: (b, jnp.maximum(i * per - 1, 0), 0))
    nxt = pl.BlockSpec((nb, w, kw), lambda b, i: (b, jnp.minimum((i + 1) * per, last), 0))
    in_specs = [cur(qw), prev, cur(kw), nxt, prev, cur(kw), nxt]
    args = [q, k, k, k, v, v, v]
    if sink is not None:
        in_specs.append(pl.BlockSpec(memory_space=pltpu.SMEM))
        args.append(sink)
    out_shape = [jax.ShapeDtypeStruct((n, seq_len, qw), BF16)]
    out_specs = [cur(qw)]
    if want_lse:
        out_shape.append(jax.ShapeDtypeStruct((n, seq_len, LANES), F32))
        out_specs.append(cur(LANES))
    return pl.pallas_call(
        functools.partial(_attn_kernel, w=w, seq_len=seq_len, heads=heads, use_sink=sink is not None,
                          want_lse=want_lse, sub_rows=sub_rows),
        grid=(n // nb, seq_len // tr),
        in_specs=in_specs,
        out_specs=out_specs,
        out_shape=out_shape,
        compiler_params=pltpu.CompilerParams(dimension_semantics=("arbitrary", "arbitrary"),
                                             vmem_limit_bytes=VMEM_LIMIT),
        name=name,
    )(*args)


_B_HEADS = tuple(tuple((h % 2, 0 if (h // 3) == (h % 2) else 1) for h in (2 * j, 2 * j + 1)) for j in range(3))
_C_HEADS = (((0, 0), (1, 0)),)


def _outffn_kernel(x_ref, of_ref, ob_ref, gate_ref, ag_ref, yb_ref,
                   o1_ref, l1_ref, o4_ref, l4_ref, o16_ref, l16_ref,
                   wout_ref, n2_ref, wg_ref, wu_ref, wd_ref,
                   y_ref, s4o_ref, s4l_ref, s16o_ref, s16l_ref):
    tm = x_ref.shape[1]
    bd = _head_mean_matrix(LANES)
    ya = []
    for j in range(A_WIDTH // LANES):
        cols = slice(j * LANES, (j + 1) * LANES)
        o = of_ref[0, :, cols] + ob_ref[0, :, cols]
        ya.append(_head_rmsnorm(o, ag_ref[...], bd) * _silu(gate_ref[0, :, cols]))
    ya = jnp.concatenate(ya, axis=1).astype(BF16)

    def interleave(o_ref, l_ref, so_ref, sl_ref, d):
        for r in range(d):
            so_ref[pl.ds(r, tm // d, stride=d), :] = o_ref[0, r].astype(F32)
            sl_ref[pl.ds(r, tm // d, stride=d), :] = l_ref[0, r]
        return so_ref[...], sl_ref[...]

    o1, l1 = o1_ref[0].astype(F32), l1_ref[0]
    o4, l4 = interleave(o4_ref, l4_ref, s4o_ref, s4l_ref, 4)
    o16, l16 = interleave(o16_ref, l16_ref, s16o_ref, s16l_ref, 16)
    m = jnp.maximum(jnp.maximum(l1, l4), l16)
    e1, e4, e16 = jnp.exp2(l1 - m), jnp.exp2(l4 - m), jnp.exp2(l16 - m)
    yc = ((e1 * o1 + e4 * o4 + e16 * o16) / (e1 + e4 + e16)).astype(BF16)

    h = x_ref[0] + _dot(jnp.concatenate([ya, yb_ref[0], yc], axis=1), wout_ref[...])
    hn = (h * lax.rsqrt(jnp.mean(h * h, axis=-1, keepdims=True) + NORM_EPS) * n2_ref[...]).astype(BF16)

    y = h
    c0 = 0
    for width in FF_GROUPS:
        cols = slice(c0, c0 + width)
        act = (_silu(_dot(hn, wg_ref[:, cols])) * _dot(hn, wu_ref[:, cols])).astype(BF16)
        y = y + _dot(act, wd_ref[cols, :])
        c0 += width
    y_ref[0] = y


def _outffn_call(x, o_f, o_b, a_raw, ag, yb, c_out, w_out, n2, wg, wu, wd):
    n, t, _ = x.shape
    tm = OUT_TOKEN_TILE
    tok = lambda width, cb=0: pl.BlockSpec((1, tm, width), lambda b, i: (b, i, cb))
    dil = lambda d: pl.BlockSpec((1, d, tm // d, LANES), lambda b, i: (b, 0, i, 0))
    const = lambda shape: pl.BlockSpec(shape, lambda b, i: (0,) * len(shape), pipeline_mode=pl.Buffered(1))
    (o1, l1), (o4, l4), (o16, l16) = c_out
    return pl.pallas_call(
        _outffn_kernel,
        grid=(n, t // tm),
        in_specs=[tok(D_MODEL), tok(A_WIDTH), tok(A_WIDTH), tok(A_WIDTH, 4), const((1, LANES)), tok(3 * LANES),
                  tok(LANES), tok(LANES), dil(4), dil(4), dil(16), dil(16),
                  const(w_out.shape), const((1, D_MODEL)), const(wg.shape), const(wu.shape), const(wd.shape)],
        out_specs=tok(D_MODEL),
        out_shape=jax.ShapeDtypeStruct((n, t, D_MODEL), F32),
        scratch_shapes=[pltpu.VMEM((tm, LANES), F32)] * 4,
        compiler_params=pltpu.CompilerParams(dimension_semantics=("arbitrary", "arbitrary"),
                                             vmem_limit_bytes=VMEM_LIMIT),
        name="outffn",
    )(x, o_f, o_b, a_raw, ag, yb, o1, l1, o4, l4, o16, l16, w_out, n2, wg, wu, wd)


def _layer(x, p, layer, cos, sin):
    n, t, _ = x.shape
    outs = _proj_call(x, p["n1"], p["w_in"], cos, sin, p["bqg"], p["bkg"], p["cqg"], p["ckg"])
    a_raw, bq, bk, bv = outs[:4]
    o_f, o_b = _hgrn_call(a_raw, p["lb_raw"], layer)
    (yb,) = _attn_call(bq, bk, bv, p["sink"], w=B_HALF_WINDOW, heads=_B_HEADS, want_lse=False,
                       rows_per_step=ATTN_ROWS_WINDOW, sub_rows=ATTN_SUB_ROWS_WINDOW, name="attn_window")
    c_out = []
    for i, d in enumerate(C_DILATIONS):
        q, k, v = (a.reshape(n * d, t // d, LANES) for a in outs[4 + 3 * i:7 + 3 * i])
        o, lse = _attn_call(q, k, v, None, w=C_HALF_WINDOW, heads=_C_HEADS, want_lse=True,
                            rows_per_step=ATTN_ROWS_DILATED, sub_rows=ATTN_SUB_ROWS_DILATED, name=f"attn_dil{d}")
        shape = (n, t, LANES) if d == 1 else (n, d, t // d, LANES)
        c_out.append((o.reshape(shape), lse.reshape(shape)))
    return _outffn_call(x, o_f, o_b, a_raw, p["ag"], yb, c_out, p["w_out"], p["n2"], p["wg"], p["wu"], p["wd"])


def _layer_params(l, norm1_g, w_in, lb_raw, a_norm_g, b_qn_g, b_kn_g, b_sink, c_qn_g, c_kn_g, w_out, norm2_g,
                  w_gate, w_up, w_down):
    pair = lambda g: jnp.tile(g[l].astype(F32), LANES // HEAD_DIM)[None, :]
    return {
        "n1": norm1_g[l].astype(F32)[None, :],
        "w_in": w_in[l].astype(BF16),
        "lb_raw": lb_raw.astype(F32).reshape(lb_raw.shape[0], 2 * A_WIDTH),
        "ag": pair(a_norm_g), "bqg": pair(b_qn_g), "bkg": pair(b_kn_g), "cqg": pair(c_qn_g), "ckg": pair(c_kn_g),
        "sink": b_sink[l].astype(F32),
        "w_out": w_out[l].astype(BF16),
        "n2": norm2_g[l].astype(F32)[None, :],
        "wg": w_gate[l].astype(BF16),
        "wu": w_up[l].astype(BF16),
        "wd": w_down[l].astype(BF16),
    }


def kernel(x_prompt, x_sample, norm1_g, w_in, lb_raw, a_norm_g, b_qn_g, b_kn_g, b_sink, c_qn_g, c_kn_g, w_out,
           norm2_g, w_gate, w_up, w_down):
    depth = w_in.shape[0]
    params = [_layer_params(l, norm1_g, w_in, lb_raw, a_norm_g, b_qn_g, b_kn_g, b_sink, c_qn_g, c_kn_g, w_out,
                            norm2_g, w_gate, w_up, w_down) for l in range(depth)]

    def trunk(x):
        cos, sin = _rope_tables(x.shape[1])
        for l in range(depth):
            x = _layer(x, params[l], l, cos, sin)
        return x

    return (trunk(x_prompt), trunk(x_sample))
```

```python
import functools

import jax
import jax.numpy as jnp
from jax import lax
from jax.experimental import pallas as pl
from jax.experimental.pallas import tpu as pltpu

F32 = jnp.float32
BF16 = jnp.bfloat16

D_MODEL = 1024
HEAD_DIM = 64
HEAD_SHIFT = HEAD_DIM.bit_length() - 1
LANES = 128
A_WIDTH = 256
A_COLS = 5 * A_WIDTH
BQ_COL, BK_COL, BV_COL = 1280, 1664, 1792
CQ_COL, CK_COL, CV_COL = 1920, 2304, 2688
B_HALF_WINDOW = 128
C_HALF_WINDOW = 64
C_DILATIONS = (1, 4, 16)
D_FF = 2816
FF_GROUPS = (1024, 1024, 768)
ROPE_THETA = 10000.0
NORM_EPS = 1e-6
MASK_VALUE = -1e30
GATE_FLOOR = 1e-30
GLA_CHUNK = 64
GLA_SAFE_RANGE = 110.0

TOKEN_TILE = 1024
OUT_TOKEN_TILE = 512
HGRN_TILE = 2048
ATTN_ROWS_WINDOW = 2048
ATTN_ROWS_DILATED = 4096
HGRN_STAGE_SKEW = 2
ATTN_SUB_ROWS_WINDOW = 128
ATTN_SUB_ROWS_DILATED = 128
ATTN_STAGE_SKEW = 1
LOG2_E = 1.4426950408889634
VMEM_LIMIT = 56 * 1024 * 1024


def _dot(a, b):
    return jnp.dot(a, b, preferred_element_type=F32)


def _dot_nt(a, b):
    return lax.dot_general(a, b, (((1,), (1,)), ((), ())), preferred_element_type=F32)


def _dot_tn(a, b):
    return lax.dot_general(a, b, (((0,), (0,)), ((), ())), preferred_element_type=F32)


def _head_blockdiag(n):
    r = lax.broadcasted_iota(jnp.int32, (n, n), 0) >> HEAD_SHIFT
    c = lax.broadcasted_iota(jnp.int32, (n, n), 1) >> HEAD_SHIFT
    return (r == c).astype(BF16)


def _head_mean_matrix(n):
    bd = _head_blockdiag(n) * (1.0 / HEAD_DIM)
    return jnp.concatenate([bd, bd], axis=0).astype(BF16)


def _head_mean(x, mean2):
    hi = x.astype(BF16)
    lo = (x - hi.astype(F32)).astype(BF16)
    return _dot(jnp.concatenate([hi, lo], axis=1), mean2)


def _head_rmsnorm(y, g, mean2):
    return y * lax.rsqrt(_head_mean(y * y, mean2) + NORM_EPS) * g


def _silu(x):
    half = 0.5 * x
    return half * jnp.tanh(half) + half


def _proj_kernel(x_ref, g1_ref, w_ref, cos_ref, sin_ref, bqg_ref, bkg_ref, cqg_ref, ckg_ref,
                 a_ref, bq_ref, bk_ref, bv_ref,
                 cq1_ref, ck1_ref, cv1_ref, cq4_ref, ck4_ref, cv4_ref, cq16_ref, ck16_ref, cv16_ref,
                 scr_ref):
    tm = x_ref.shape[1]
    x = x_ref[0]
    xn = (x * lax.rsqrt(jnp.mean(x * x, axis=-1, keepdims=True) + NORM_EPS) * g1_ref[...]).astype(BF16)
    y_bc = _dot(xn, w_ref[:, A_COLS:])

    bd = _head_mean_matrix(LANES)
    cos = cos_ref[...]
    sin = sin_ref[...]
    lane = lax.broadcasted_iota(jnp.int32, (1, LANES), 1)
    first_half = (lane & (HEAD_DIM - 1)) < (HEAD_DIM // 2)

    def proj(col):
        return y_bc[:, col - A_COLS:col - A_COLS + LANES]

    def norm_rope(col, gain):
        y = _head_rmsnorm(proj(col), gain, bd)
        partner = jnp.where(first_half, pltpu.roll(y, LANES - HEAD_DIM // 2, 1), pltpu.roll(y, HEAD_DIM // 2, 1))
        return y * cos + partner * sin

    scale = LOG2_E * HEAD_DIM ** -0.5
    bq_gain, bk_gain = bqg_ref[...] * scale, bkg_ref[...]
    cq_gain, ck_gain = cqg_ref[...] * scale, ckg_ref[...]
    for j in range(3):
        bq_ref[0, :, j * LANES:(j + 1) * LANES] = norm_rope(BQ_COL + j * LANES, bq_gain).astype(BF16)
    bk = norm_rope(BK_COL, bk_gain)
    bk_ref[0, :, 0:LANES] = bk.astype(BF16)
    bk_ref[0, :, LANES:2 * LANES] = pltpu.roll(bk, HEAD_DIM, 1).astype(BF16)
    bv = proj(BV_COL)
    bv_ref[0, :, 0:LANES] = bv.astype(BF16)
    bv_ref[0, :, LANES:2 * LANES] = pltpu.roll(bv, HEAD_DIM, 1).astype(BF16)

    def store_dilated(ref, y, d, slot):
        if d == 1:
            ref[0] = y.astype(BF16)
            return
        scr_ref[slot] = y
        for r in range(d):
            ref[0, r] = scr_ref[slot, pl.ds(r, tm // d, stride=d), :].astype(BF16)

    c_refs = ((cq1_ref, ck1_ref, cv1_ref), (cq4_ref, ck4_ref, cv4_ref), (cq16_ref, ck16_ref, cv16_ref))
    for p, d in enumerate(C_DILATIONS):
        q_ref, k_ref, v_ref = c_refs[p]
        store_dilated(q_ref, norm_rope(CQ_COL + p * LANES, cq_gain), d, 3 * p)
        store_dilated(k_ref, norm_rope(CK_COL + p * LANES, ck_gain), d, 3 * p + 1)
        store_dilated(v_ref, proj(CV_COL + p * LANES), d, 3 * p + 2)

    a_ref[0] = _dot(xn, w_ref[:, 0:A_COLS])


def _rope_tables(t_len):
    half = HEAD_DIM // 2
    inv = ROPE_THETA ** (-jnp.arange(half, dtype=F32) / half)
    ang = jnp.arange(t_len).astype(F32)[:, None] * inv[None, :]
    cos = jnp.tile(jnp.cos(ang), (1, 2 * LANES // HEAD_DIM))
    sin = jnp.sin(ang)
    sin = jnp.tile(jnp.concatenate([-sin, sin], axis=1), (1, LANES // HEAD_DIM))
    return cos, sin


def _proj_call(x, g1, w_in, cos, sin, bqg, bkg, cqg, ckg):
    n, t, _ = x.shape
    tm = TOKEN_TILE
    grid = (n, t // tm)
    tok = lambda width: pl.BlockSpec((1, tm, width), lambda b, i: (b, i, 0))
    const = lambda shape: pl.BlockSpec(shape, lambda b, i: (0,) * len(shape), pipeline_mode=pl.Buffered(1))
    dil = lambda d: pl.BlockSpec((1, d, tm // d, LANES), lambda b, i: (b, 0, i, 0))
    out_shape = [jax.ShapeDtypeStruct((n, t, A_COLS), F32),
                 jax.ShapeDtypeStruct((n, t, 3 * LANES), BF16),
                 jax.ShapeDtypeStruct((n, t, 2 * LANES), BF16),
                 jax.ShapeDtypeStruct((n, t, 2 * LANES), BF16)]
    out_specs = [tok(A_COLS), tok(3 * LANES), tok(2 * LANES), tok(2 * LANES)]
    for d in C_DILATIONS:
        for _ in range(3):
            if d == 1:
                out_shape.append(jax.ShapeDtypeStruct((n, t, LANES), BF16))
                out_specs.append(tok(LANES))
            else:
                out_shape.append(jax.ShapeDtypeStruct((n, d, t // d, LANES), BF16))
                out_specs.append(dil(d))
    rope_spec = pl.BlockSpec((tm, LANES), lambda b, i: (i, 0))
    return pl.pallas_call(
        _proj_kernel,
        grid=grid,
        in_specs=[tok(D_MODEL), const((1, D_MODEL)), const(w_in.shape), rope_spec, rope_spec,
                  const((1, LANES)), const((1, LANES)), const((1, LANES)), const((1, LANES))],
        out_specs=out_specs,
        out_shape=out_shape,
        scratch_shapes=[pltpu.VMEM((3 * len(C_DILATIONS), tm, LANES), F32)],
        compiler_params=pltpu.CompilerParams(dimension_semantics=("arbitrary", "arbitrary"),
                                             vmem_limit_bytes=VMEM_LIMIT),
        name="proj",
    )(x, g1, w_in, cos, sin, bqg, bkg, cqg, ckg)


def _hgrn_gates(u):
    c = GLA_CHUNK
    reverse, lb = u["reverse"], u["lb"]
    q, f_raw, v = u["load"]()
    half_t = 0.5 * jnp.tanh(0.5 * f_raw)
    f = lb + (1.0 - lb) * (0.5 + half_t)
    g = jnp.log2(jnp.maximum(f, GATE_FLOOR))
    kk = (1.0 - lb) * (0.5 - half_t)

    row = lax.broadcasted_iota(jnp.int32, (c, 2 * c), 0)
    col = lax.broadcasted_iota(jnp.int32, (c, 2 * c), 1) & (c - 1)
    tri2 = ((col >= row) if reverse else (col <= row)).astype(BF16)
    g1 = g.astype(BF16)
    g2 = (g - g1.astype(F32)).astype(BF16)
    b = _dot(tri2, jnp.concatenate([g1, g2], axis=0))
    u.update(q=q, v=v, kk=kk, b=b)


def _hgrn_state(u):
    c = GLA_CHUNK
    reverse, state_ref = u["reverse"], u["state_ref"]
    q, v, kk, b = u["q"], u["v"], u["kk"], u["b"]
    total = b[0:1] if reverse else b[c - 1:c]
    mid = b[c // 2:c // 2 + 1] if reverse else b[c // 2 - 1:c // 2]
    v16 = v.astype(BF16)

    qs = (q * jnp.exp2(b)).astype(BF16)
    ks = (kk * jnp.exp2(total - b)).astype(BF16)
    decay = jnp.exp2(total)
    srow = lax.broadcasted_iota(jnp.int32, (LANES, LANES), 0) >> HEAD_SHIFT
    scol = lax.broadcasted_iota(jnp.int32, (LANES, LANES), 1) >> HEAD_SHIFT
    same_head = srow == scol
    o = []
    for j in range(A_WIDTH // LANES):
        cols = slice(j * LANES, (j + 1) * LANES)
        state = state_ref[j]
        o.append(_dot_nt(qs[:, cols], state.astype(BF16)))
        upd = _dot_tn(v16[:, cols], ks[:, cols])
        state_ref[j] = state * decay[:, cols] + jnp.where(same_head, upd, 0.0)

    offset = jnp.maximum(jnp.abs(b[0:1] - mid), jnp.abs(b[c - 1:c] - mid))
    u.update(o=o, v16=v16, offset=offset)

    if u["exact_refs"] is None:
        qp = q * jnp.exp2(b - mid)
        kp = (kk * jnp.exp2(mid - b)).astype(BF16)
        lane_half = lax.broadcasted_iota(jnp.int32, (1, LANES), 1) >> HEAD_SHIFT
        t_idx = lax.broadcasted_iota(jnp.int32, (2 * c, c), 0) & (c - 1)
        s_idx = lax.broadcasted_iota(jnp.int32, (2 * c, c), 1)
        causal = (s_idx >= t_idx) if reverse else (s_idx <= t_idx)
        u["a16"] = []
        for j in range(A_WIDTH // LANES):
            cols = slice(j * LANES, (j + 1) * LANES)
            q_stack = jnp.concatenate([jnp.where(lane_half == h, qp[:, cols], 0.0) for h in range(2)],
                                      axis=0).astype(BF16)
            a = _dot_nt(q_stack, kp[:, cols])
            u["a16"].append(jnp.where(causal, a, 0.0).astype(BF16))


def _hgrn_output(u):
    c = GLA_CHUNK
    reverse = u["reverse"]
    if u["exact_refs"] is None:
        lane_half = lax.broadcasted_iota(jnp.int32, (1, LANES), 1) >> HEAD_SHIFT
        o = []
        for j in range(A_WIDTH // LANES):
            v_blk = u["v16"][:, j * LANES:(j + 1) * LANES]
            acc = u["o"][j]
            for h in range(2):
                acc = acc + _dot(u["a16"][j][h * c:(h + 1) * c], jnp.where(lane_half == h, v_blk, 0))
            o.append(acc)
        o = jnp.concatenate(o, axis=1)
    else:
        q, b = u["q"], u["b"]
        tb_ref, tk_ref, tv_ref = u["exact_refs"]
        tb_ref[...] = b
        tk_ref[...] = u["kk"]
        tv_ref[...] = u["v"]
        bd = _head_blockdiag(A_WIDTH)
        t_col = lax.broadcasted_iota(jnp.int32, (c, 1), 0)

        def body(si, acc):
            b_s = tb_ref[pl.ds(si, 1), :]
            dec = jnp.exp2(jnp.minimum(b - b_s, 0.0))
            valid = (t_col <= si) if reverse else (t_col >= si)
            p = jnp.where(valid, q * dec * tk_ref[pl.ds(si, 1), :], 0.0)
            return acc + _dot(p.astype(BF16), bd) * tv_ref[pl.ds(si, 1), :]

        o = lax.fori_loop(0, c, body, jnp.concatenate(u["o"], axis=1))
    u["store"](o)


def _hgrn_kernel(lb_ref, qf_ref, ff_ref, vf_ref, qb_ref, fb_ref, vb_ref, of_ref, ob_ref,
                 sf_ref, sb_ref, sf0_ref, sb0_ref, tb_ref, tk_ref, tv_ref, *, layer, n_chunks):
    @pl.when(pl.program_id(1) == 0)
    def _():
        sf_ref[...] = jnp.zeros_like(sf_ref)
        sb_ref[...] = jnp.zeros_like(sb_ref)

    sf0_ref[...] = sf_ref[...]
    sb0_ref[...] = sb_ref[...]

    raw = lb_ref[...]
    depth = raw.shape[0]
    mx = raw[0:1]
    for i in range(1, depth):
        mx = jnp.maximum(mx, raw[i:i + 1])
    e = jnp.exp(raw - mx)
    den = e[0:1]
    for i in range(1, depth):
        den = den + e[i:i + 1]
    sm = e / den
    cs = sm[0:1]
    for i in range(1, layer + 1):
        cs = cs + sm[i:i + 1]
    lb = cs - sm[0:1]
    lb_f = lb[:, 0:A_WIDTH]
    lb_b = lb[:, A_WIDTH:2 * A_WIDTH]

    def chunk_rows(ci):
        start = ci * GLA_CHUNK
        return pl.ds(start if isinstance(ci, int) else pl.multiple_of(start, GLA_CHUNK), GLA_CHUNK)

    def chunk_pair(ci, exact_refs):
        def record(q_ref, f_ref, v_ref, o_ref, rows, lb_d, state_ref, reverse):
            def store(o):
                o_ref[0, rows, :] = o
            return {"load": lambda: (q_ref[0, rows, :], f_ref[0, rows, :], v_ref[0, rows, :]), "store": store,
                    "lb": lb_d, "state_ref": state_ref, "reverse": reverse, "exact_refs": exact_refs}
        return [record(qf_ref, ff_ref, vf_ref, of_ref, chunk_rows(ci), lb_f, sf_ref, False),
                record(qb_ref, fb_ref, vb_ref, ob_ref, chunk_rows(n_chunks - 1 - ci), lb_b, sb_ref, True)]

    units = [u for ci in range(n_chunks) for u in chunk_pair(ci, None)]
    stages = (_hgrn_gates, _hgrn_state, _hgrn_output)
    skew = HGRN_STAGE_SKEW
    for i in range(len(units) + skew * (len(stages) - 1)):
        for k, stage in enumerate(stages):
            if 0 <= i - k * skew < len(units):
                stage(units[i - k * skew])
    worst = units[0]["offset"]
    for u in units[1:]:
        worst = jnp.maximum(worst, u["offset"])

    @pl.when(jnp.max(worst) > GLA_SAFE_RANGE)
    def _():
        sf_ref[...] = sf0_ref[...]
        sb_ref[...] = sb0_ref[...]

        def step(ci, carry):
            for u in chunk_pair(ci, (tb_ref, tk_ref, tv_ref)):
                for stage in stages:
                    stage(u)
            return carry

        lax.fori_loop(0, n_chunks, step, 0)


def _hgrn_call(a_raw, lb_raw2, layer):
    n, t, _ = a_raw.shape
    tt = HGRN_TILE
    nt = t // tt
    fwd = lambda cb: pl.BlockSpec((1, tt, A_WIDTH), lambda b, i: (b, i, cb))
    bwd = lambda cb: pl.BlockSpec((1, tt, A_WIDTH), lambda b, i: (b, nt - 1 - i, cb))
    return pl.pallas_call(
        functools.partial(_hgrn_kernel, layer=layer, n_chunks=tt // GLA_CHUNK),
        grid=(n, nt),
        in_specs=[pl.BlockSpec(lb_raw2.shape, lambda b, i: (0, 0)),
                  fwd(0), fwd(1), fwd(3), bwd(0), bwd(2), bwd(3)],
        out_specs=[fwd(0), bwd(0)],
        out_shape=[jax.ShapeDtypeStruct((n, t, A_WIDTH), F32)] * 2,
        scratch_shapes=[pltpu.VMEM((A_WIDTH // LANES, LANES, LANES), F32)] * 4
        + [pltpu.VMEM((GLA_CHUNK, A_WIDTH), F32)] * 3,
        compiler_params=pltpu.CompilerParams(dimension_semantics=("arbitrary", "arbitrary"),
                                             vmem_limit_bytes=VMEM_LIMIT),
        name="hgrn",
    )(lb_raw2, a_raw, a_raw, a_raw, a_raw, a_raw, a_raw)


def _attn_kernel(*refs, w, seq_len, heads, use_sink, want_lse, sub_rows):
    q_ref, kp_ref, kc_ref, kn_ref, vp_ref, vc_ref, vn_ref = refs[:7]
    rest = list(refs[7:])
    sink_ref = rest.pop(0) if use_sink else None
    o_ref = rest.pop(0)
    lse_ref = rest.pop(0) if want_lse else None

    n_seq, tr = q_ref.shape[0], q_ref.shape[1]
    sb_rows, sb_keys = sub_rows, sub_rows + 2 * w
    n_sb = tr // sb_rows
    t0 = pl.program_id(1) * tr
    lane_half = lax.broadcasted_iota(jnp.int32, (1, LANES), 1) >> HEAD_SHIFT
    row = lax.broadcasted_iota(jnp.int32, (sb_rows, sb_keys), 0)
    col = lax.broadcasted_iota(jnp.int32, (sb_rows, sb_keys), 1)
    band = (col >= row) & (col - row <= 2 * w)

    kv_cache = {}

    def kv_variants(sq):
        if sq not in kv_cache:
            k_ext = jnp.concatenate([kp_ref[sq], kc_ref[sq], kn_ref[sq]], axis=0)
            v_ext = jnp.concatenate([vp_ref[sq], vc_ref[sq], vn_ref[sq]], axis=0)
            kv = {}
            for blk_heads in heads:
                for half, kv_blk in blk_heads:
                    if (half, kv_blk) not in kv:
                        sel = lane_half == half
                        kv[(half, kv_blk)] = (jnp.where(sel, k_ext[:, kv_blk * LANES:(kv_blk + 1) * LANES], 0),
                                              jnp.where(sel, v_ext[:, kv_blk * LANES:(kv_blk + 1) * LANES], 0))
            kv_cache[sq] = kv
        return kv_cache[sq]

    mask_cache = {}

    def tile_mask(sb):
        if sb not in mask_cache:
            mask = band
            if sb == 0 or sb == n_sb - 1:
                key_pos = col + (t0 - w + sb * sb_rows)
                mask = band & (key_pos >= 0) & (key_pos < seq_len)
            mask_cache[sb] = mask
        return mask_cache[sb]

    def stage_scores(u):
        sq, sb, j = u["id"]
        r0 = sb * sb_rows
        q = q_ref[sq, r0:r0 + sb_rows, j * LANES:(j + 1) * LANES]
        u["s"], u["m"], u["sink"] = [], [], []
        for e, (half, kv_blk) in enumerate(heads[j]):
            k_h = kv_variants(sq)[(half, kv_blk)][0]
            s = jnp.where(tile_mask(sb), _dot_nt(q, k_h[r0:r0 + sb_keys]), MASK_VALUE)
            m = jnp.max(s, axis=-1, keepdims=True)
            sink = None
            if use_sink:
                sink = sink_ref[2 * j + e] * LOG2_E
                m = jnp.maximum(m, sink)
            u["s"].append(s)
            u["m"].append(m)
            u["sink"].append(sink)

    def stage_probs(u):
        u["p"], u["den"] = [], []
        for s, m, sink in zip(u["s"], u["m"], u["sink"]):
            p = jnp.exp2(s - m)
            den = jnp.sum(p, axis=-1, keepdims=True)
            if use_sink:
                den = den + jnp.exp2(sink - m)
            u["p"].append(p.astype(BF16))
            u["den"].append(den)
        u["s"] = None

    def stage_out(u):
        sq, sb, j = u["id"]
        r0 = sb * sb_rows
        out = None
        lse_out = None
        for e, (half, kv_blk) in enumerate(heads[j]):
            v_h = kv_variants(sq)[(half, kv_blk)][1]
            o = _dot(u["p"][e], v_h[r0:r0 + sb_keys]) * (1.0 / u["den"][e])
            out = o if out is None else out + o
            if want_lse:
                lse = u["m"][e] + jnp.log2(u["den"][e])
                lse_out = lse if lse_out is None else jnp.where(lane_half == half, lse, lse_out)
        o_ref[sq, r0:r0 + sb_rows, j * LANES:(j + 1) * LANES] = out.astype(o_ref.dtype)
        if want_lse:
            lse_ref[sq, r0:r0 + sb_rows, :] = jnp.broadcast_to(lse_out, (sb_rows, LANES))

    units = [{"id": (sq, sb, j)} for sq in range(n_seq) for sb in range(n_sb) for j in range(len(heads))]
    skew = ATTN_STAGE_SKEW
    for i in range(len(units) + 2 * skew):
        if i < len(units):
            stage_scores(units[i])
        if 0 <= i - skew < len(units):
            stage_probs(units[i - skew])
        if 0 <= i - 2 * skew < len(units):
            stage_out(units[i - 2 * skew])


def _attn_call(q, k, v, sink, *, w, heads, want_lse, rows_per_step, sub_rows, name):
    n, seq_len, qw = q.shape
    kw = k.shape[-1]
    tr = min(rows_per_step, seq_len)
    nb = min(rows_per_step // tr, n)
    sub_rows = min(sub_rows, tr)
    assert seq_len % tr == 0 and n % nb == 0 and tr % sub_rows == 0 and tr % w == 0
    per = tr // w
    last = seq_len // w - 1
    cur = lambda width: pl.BlockSpec((nb, tr, width), lambda b, i: (b, i, 0))
    prev = pl.BlockSpec((nb, w, kw), lambda b, i: (b, jnp.maximum(i * per - 1, 0), 0))
    nxt = pl.BlockSpec((nb, w, kw), lambda b, i: (b, jnp.minimum((i + 1) * per, last), 0))
    in_specs = [cur(qw), prev, cur(kw), nxt, prev, cur(kw), nxt]
    args = [q, k, k, k, v, v, v]
    if sink is not None:
        in_specs.append(pl.BlockSpec(memory_space=pltpu.SMEM))
        args.append(sink)
    out_shape = [jax.ShapeDtypeStruct((n, seq_len, qw), BF16)]
    out_specs = [cur(qw)]
    if want_lse:
        out_shape.append(jax.ShapeDtypeStruct((n, seq_len, LANES), F32))
        out_specs.append(cur(LANES))
    return pl.pallas_call(
        functools.partial(_attn_kernel, w=w, seq_len=seq_len, heads=heads, use_sink=sink is not None,
                          want_lse=want_lse, sub_rows=sub_rows),
        grid=(n // nb, seq_len // tr),
        in_specs=in_specs,
        out_specs=out_specs,
        out_shape=out_shape,
        compiler_params=pltpu.CompilerParams(dimension_semantics=("arbitrary", "arbitrary"),
                                             vmem_limit_bytes=VMEM_LIMIT),
        name=name,
    )(*args)


_B_HEADS = tuple(tuple((h % 2, 0 if (h // 3) == (h % 2) else 1) for h in (2 * j, 2 * j + 1)) for j in range(3))
_C_HEADS = (((0, 0), (1, 0)),)


def _outffn_kernel(x_ref, of_ref, ob_ref, gate_ref, ag_ref, yb_ref,
                   o1_ref, l1_ref, o4_ref, l4_ref, o16_ref, l16_ref,
                   wout_ref, n2_ref, wg_ref, wu_ref, wd_ref,
                   y_ref, s4o_ref, s4l_ref, s16o_ref, s16l_ref):
    tm = x_ref.shape[1]
    bd = _head_mean_matrix(LANES)
    ya = []
    for j in range(A_WIDTH // LANES):
        cols = slice(j * LANES, (j + 1) * LANES)
        o = of_ref[0, :, cols] + ob_ref[0, :, cols]
        ya.append(_head_rmsnorm(o, ag_ref[...], bd) * _silu(gate_ref[0, :, cols]))
    ya = jnp.concatenate(ya, axis=1).astype(BF16)

    def interleave(o_ref, l_ref, so_ref, sl_ref, d):
        for r in range(d):
            so_ref[pl.ds(r, tm // d, stride=d), :] = o_ref[0, r].astype(F32)
            sl_ref[pl.ds(r, tm // d, stride=d), :] = l_ref[0, r]
        return so_ref[...], sl_ref[...]

    o1, l1 = o1_ref[0].astype(F32), l1_ref[0]
    o4, l4 = interleave(o4_ref, l4_ref, s4o_ref, s4l_ref, 4)
    o16, l16 = interleave(o16_ref, l16_ref, s16o_ref, s16l_ref, 16)
    m = jnp.maximum(jnp.maximum(l1, l4), l16)
    e1, e4, e16 = jnp.exp2(l1 - m), jnp.exp2(l4 - m), jnp.exp2(l16 - m)
    yc = ((e1 * o1 + e4 * o4 + e16 * o16) / (e1 + e4 + e16)).astype(BF16)

    h = x_ref[0] + _dot(jnp.concatenate([ya, yb_ref[0], yc], axis=1), wout_ref[...])
    hn = (h * lax.rsqrt(jnp.mean(h * h, axis=-1, keepdims=True) + NORM_EPS) * n2_ref[...]).astype(BF16)

    y = h
    c0 = 0
    for width in FF_GROUPS:
        cols = slice(c0, c0 + width)
        act = (_silu(_dot(hn, wg_ref[:, cols])) * _dot(hn, wu_ref[:, cols])).astype(BF16)
        y = y + _dot(act, wd_ref[cols, :])
        c0 += width
    y_ref[0] = y


def _outffn_call(x, o_f, o_b, a_raw, ag, yb, c_out, w_out, n2, wg, wu, wd):
    n, t, _ = x.shape
    tm = OUT_TOKEN_TILE
    tok = lambda width, cb=0: pl.BlockSpec((1, tm, width), lambda b, i: (b, i, cb))
    dil = lambda d: pl.BlockSpec((1, d, tm // d, LANES), lambda b, i: (b, 0, i, 0))
    const = lambda shape: pl.BlockSpec(shape, lambda b, i: (0,) * len(shape), pipeline_mode=pl.Buffered(1))
    (o1, l1), (o4, l4), (o16, l16) = c_out
    return pl.pallas_call(
        _outffn_kernel,
        grid=(n, t // tm),
        in_specs=[tok(D_MODEL), tok(A_WIDTH), tok(A_WIDTH), tok(A_WIDTH, 4), const((1, LANES)), tok(3 * LANES),
                  tok(LANES), tok(LANES), dil(4), dil(4), dil(16), dil(16),
                  const(w_out.shape), const((1, D_MODEL)), const(wg.shape), const(wu.shape), const(wd.shape)],
        out_specs=tok(D_MODEL),
        out_shape=jax.ShapeDtypeStruct((n, t, D_MODEL), F32),
        scratch_shapes=[pltpu.VMEM((tm, LANES), F32)] * 4,
        compiler_params=pltpu.CompilerParams(dimension_semantics=("arbitrary", "arbitrary"),
                                             vmem_limit_bytes=VMEM_LIMIT),
        name="outffn",
    )(x, o_f, o_b, a_raw, ag, yb, o1, l1, o4, l4, o16, l16, w_out, n2, wg, wu, wd)


def _layer(x, p, layer, cos, sin):
    n, t, _ = x.shape
    outs = _proj_call(x, p["n1"], p["w_in"], cos, sin, p["bqg"], p["bkg"], p["cqg"], p["ckg"])
    a_raw, bq, bk, bv = outs[:4]
    o_f, o_b = _hgrn_call(a_raw, p["lb_raw"], layer)
    (yb,) = _attn_call(bq, bk, bv, p["sink"], w=B_HALF_WINDOW, heads=_B_HEADS, want_lse=False,
                       rows_per_step=ATTN_ROWS_WINDOW, sub_rows=ATTN_SUB_ROWS_WINDOW, name="attn_window")
    c_out = []
    for i, d in enumerate(C_DILATIONS):
        q, k, v = (a.reshape(n * d, t // d, LANES) for a in outs[4 + 3 * i:7 + 3 * i])
        o, lse = _attn_call(q, k, v, None, w=C_HALF_WINDOW, heads=_C_HEADS, want_lse=True,
                            rows_per_step=ATTN_ROWS_DILATED, sub_rows=ATTN_SUB_ROWS_DILATED, name=f"attn_dil{d}")
        shape = (n, t, LANES) if d == 1 else (n, d, t // d, LANES)
        c_out.append((o.reshape(shape), lse.reshape(shape)))
    return _outffn_call(x, o_f, o_b, a_raw, p["ag"], yb, c_out, p["w_out"], p["n2"], p["wg"], p["wu"], p["wd"])


def _layer_params(l, norm1_g, w_in, lb_raw, a_norm_g, b_qn_g, b_kn_g, b_sink, c_qn_g, c_kn_g, w_out, norm2_g,
                  w_gate, w_up, w_down):
    pair = lambda g: jnp.tile(g[l].astype(F32), LANES // HEAD_DIM)[None, :]
    return {
        "n1": norm1_g[l].astype(F32)[None, :],
        "w_in": w_in[l].astype(BF16),
        "lb_raw": lb_raw.astype(F32).reshape(lb_raw.shape[0], 2 * A_WIDTH),
        "ag": pair(a_norm_g), "bqg": pair(b_qn_g), "bkg": pair(b_kn_g), "cqg": pair(c_qn_g), "ckg": pair(c_kn_g),
        "sink": b_sink[l].astype(F32),
        "w_out": w_out[l].astype(BF16),
        "n2": norm2_g[l].astype(F32)[None, :],
        "wg": w_gate[l].astype(BF16),
        "wu": w_up[l].astype(BF16),
        "wd": w_down[l].astype(BF16),
    }


def kernel(x_prompt, x_sample, norm1_g, w_in, lb_raw, a_norm_g, b_qn_g, b_kn_g, b_sink, c_qn_g, c_kn_g, w_out,
           norm2_g, w_gate, w_up, w_down):
    depth = w_in.shape[0]
    params = [_layer_params(l, norm1_g, w_in, lb_raw, a_norm_g, b_qn_g, b_kn_g, b_sink, c_qn_g, c_kn_g, w_out,
                            norm2_g, w_gate, w_up, w_down) for l in range(depth)]

    def trunk(x):
        cos, sin = _rope_tables(x.shape[1])
        for l in range(depth):
            x = _layer(x, params[l], l, cos, sin)
        return x

    return (trunk(x_prompt), trunk(x_sample))
```
